```python
import math
import jax, jax.numpy as jnp
from jax import lax
import numpy as np

D_MODEL = 2048
BATCH = 2
SEQ = 4096
DEPTH = 1

CHUNK = 64
QBLOCK = 128
EPS = 1e-6

MLA_HEADS = 8
MLA_NOPE = 128
MLA_ROPE = 64
MLA_V = 128
MLA_QK = MLA_NOPE + MLA_ROPE
Q_LORA = 512
KV_LORA = 256
ROPE_THETA = 10000.0

DSA_HEADS = 8
DSA_HD = 128
IDX_HEADS = 16
IDX_HD = 64
DSA_TOPK = 256

REL_BUCKETS = 32
REL_MAX_DIST = 128

PEER_HEADS = 8
PEER_NKEYS = 128
PEER_EXPERTS = PEER_NKEYS * PEER_NKEYS
PEER_DKEY = 256
PEER_TOPK = 16

MIX_WIDTH = MLA_HEADS * MLA_V + DSA_HEADS * DSA_HD
IN_SPLITS = (Q_LORA, KV_LORA, MLA_ROPE,
             DSA_HEADS * DSA_HD, DSA_HEADS * DSA_HD, DSA_HEADS * DSA_HD,
             IDX_HEADS * IDX_HD, IDX_HD, IDX_HEADS)
IN_WIDTH = sum(IN_SPLITS)
IN_OFFSETS = tuple(int(o) for o in np.cumsum(IN_SPLITS)[:-1])

kernel_name = "hybrid_mla_dsa_peer_block"


def rmsnorm(x, g):
    xf = x.astype(jnp.float32)
    y = xf * lax.rsqrt(jnp.mean(xf * xf, axis=-1, keepdims=True) + EPS)
    return (y * g.astype(jnp.float32)).astype(x.dtype)


def rope(x, pos):
    half = MLA_ROPE // 2
    freqs = ROPE_THETA ** (-2.0 * jnp.arange(half, dtype=jnp.float32) / MLA_ROPE)
    ang = pos.astype(jnp.float32)[:, None] * freqs[None, :]
    cos = jnp.cos(ang)[None, :, None, :]
    sin = jnp.sin(ang)[None, :, None, :]
    xf = x.astype(jnp.float32)
    x1, x2 = xf[..., :half], xf[..., half:]
    return jnp.concatenate([x1 * cos - x2 * sin, x1 * sin + x2 * cos], axis=-1).astype(x.dtype)


def to_blocks(a):
    b, s = a.shape[:2]
    return jnp.moveaxis(a.reshape((b, s // QBLOCK, QBLOCK) + a.shape[2:]), 1, 0)


def from_blocks(a):
    n, b, t = a.shape[:3]
    return jnp.moveaxis(a, 0, 1).reshape((b, n * t) + a.shape[3:])


def t5_bucket(rel):
    nb = REL_BUCKETS // 2
    max_exact = nb // 2
    ret = jnp.where(rel > 0, nb, 0)
    n = jnp.abs(rel)
    nf = jnp.maximum(n, 1).astype(jnp.float32)
    large = max_exact + (jnp.log(nf / max_exact) / math.log(REL_MAX_DIST / max_exact)
                         * (nb - max_exact)).astype(jnp.int32)
    large = jnp.minimum(large, nb - 1)
    return ret + jnp.where(n < max_exact, n, large)


def mla_attention(c_q, c_kv, k_rope, g_cq, g_ckv, w_uq, w_ukv, g_qn, g_kn, pos):
    b, s, _ = c_q.shape
    q = (rmsnorm(c_q, g_cq) @ w_uq).reshape(b, s, MLA_HEADS, MLA_QK)
    kv = (rmsnorm(c_kv, g_ckv) @ w_ukv).reshape(b, s, MLA_HEADS, MLA_NOPE + MLA_V)
    k_nope, v = kv[..., :MLA_NOPE], kv[..., MLA_NOPE:]
    k = jnp.concatenate([k_nope, jnp.broadcast_to(k_rope[:, :, None, :], (b, s, MLA_HEADS, MLA_ROPE))], axis=-1)
    q = rmsnorm(q, g_qn)
    k = rmsnorm(k, g_kn)
    q = jnp.concatenate([q[..., :MLA_NOPE], rope(q[..., MLA_NOPE:], pos)], axis=-1)
    k = jnp.concatenate([k[..., :MLA_NOPE], rope(k[..., MLA_NOPE:], pos)], axis=-1)
    key_chunk = pos // CHUNK
    scale = MLA_QK ** -0.5

    def block(args):
        qb, tb = args
        sc = jnp.einsum('bthd,bshd->bhts', qb, k, preferred_element_type=jnp.float32) * scale
        mask = key_chunk[None, :] <= (tb // CHUNK)[:, None]
        sc = jnp.where(mask[None, None], sc, -jnp.inf)
        p = jax.nn.softmax(sc, axis=-1)
        return jnp.einsum('bhts,bshd->bthd', p.astype(v.dtype), v)

    out = lax.map(block, (to_blocks(q), pos.reshape(-1, QBLOCK)))
    return from_blocks(out).reshape(b, s, MLA_HEADS * MLA_V)


def dsa_attention(q, k, v, q_idx, k_idx, w_idx, g_qn, g_kn, rel_bias, pos):
    b, s, _ = q.shape
    q = rmsnorm(q.reshape(b, s, DSA_HEADS, DSA_HD), g_qn)
    k = rmsnorm(k.reshape(b, s, DSA_HEADS, DSA_HD), g_kn)
    v = v.reshape(b, s, DSA_HEADS, DSA_HD)
    q_idx = q_idx.reshape(b, s, IDX_HEADS, IDX_HD)
    n_sel = min(DSA_TOPK, s // 4)
    key_chunk = pos // CHUNK
    scale = DSA_HD ** -0.5
    idx_scale = IDX_HD ** -0.5
    gather = jax.vmap(lambda a, i: a[i])

    def block(args):
        qb, qib, wb, tb = args
        dots = jnp.einsum('bthd,bsd->bths', qib, k_idx, preferred_element_type=jnp.float32) * idx_scale
        index_score = jnp.einsum('bth,bths->bts', wb.astype(jnp.float32), jax.nn.relu(dots))
        admissible = key_chunk[None, :] <= (tb // CHUNK)[:, None]
        index_score = jnp.where(admissible[None], index_score, -jnp.inf)
        _, sel = lax.top_k(index_score, n_sel)
        valid = (sel // CHUNK) <= (tb // CHUNK)[None, :, None]
        ks = gather(k, sel)
        vs = gather(v, sel)
        bias = rel_bias[t5_bucket(sel - tb[None, :, None])]
        logits = (jnp.einsum('bthd,btkhd->bhtk', qb, ks, preferred_element_type=jnp.float32) * scale
                  + jnp.moveaxis(bias, -1, 1).astype(jnp.float32))
        logits = jnp.where(valid[:, None], logits, -jnp.inf)
        p = jax.nn.softmax(logits, axis=-1)
        return jnp.einsum('bhtk,btkhd->bthd', p.astype(vs.dtype), vs)

    out = lax.map(block, (to_blocks(q), to_blocks(q_idx), to_blocks(w_idx), pos.reshape(-1, QBLOCK)))
    return from_blocks(out).reshape(b, s, DSA_HEADS * DSA_HD)


def peer(h, w_pq, sub_keys, u_tab, v_tab):
    def block(hb):
        b, t, _ = hb.shape
        qr = (hb @ w_pq).reshape(b, t, PEER_HEADS, 2, PEER_DKEY // 2)
        sc = jnp.einsum('bthpd,hpnd->bthpn', qr, sub_keys, preferred_element_type=jnp.float32)
        s_top, i_top = lax.top_k(sc, PEER_TOPK)
        cand = s_top[..., 0, :, None] + s_top[..., 1, None, :]
        c_top, c_pos = lax.top_k(cand.reshape(b, t, PEER_HEADS, PEER_TOPK * PEER_TOPK), PEER_TOPK)
        i0 = jnp.take_along_axis(i_top[..., 0, :], c_pos // PEER_TOPK, axis=-1)
        i1 = jnp.take_along_axis(i_top[..., 1, :], c_pos % PEER_TOPK, axis=-1)
        expert = i0 * PEER_NKEYS + i1
        g = jax.nn.softmax(c_top, axis=-1)
        u = u_tab[expert]
        a = jax.nn.gelu(jnp.einsum('bthkd,btd->bthk', u, hb, preferred_element_type=jnp.float32),
                        approximate=False)
        return jnp.einsum('bthk,bthkd->btd', (g * a).astype(v_tab.dtype), v_tab[expert])

    return from_blocks(lax.map(block, to_blocks(h)))


def setup_inputs(seed: int = 0) -> dict:
    key = jax.random.key(seed)
    ks = jax.random.split(key, 20)
    f32 = jnp.float32

    def nrm(k, shape, scale):
        return jax.random.normal(k, shape, f32) * scale

    def gain(k, shape):
        return 1.0 + 0.01 * jax.random.normal(k, shape, f32)

    return {
        "x": nrm(ks[0], (BATCH, SEQ, D_MODEL), 1.0),
        "g_attn": gain(ks[1], (DEPTH, D_MODEL)),
        "w_in": nrm(ks[2], (DEPTH, D_MODEL, IN_WIDTH), D_MODEL ** -0.5),
        "g_cq": gain(ks[3], (DEPTH, Q_LORA)),
        "g_ckv": gain(ks[4], (DEPTH, KV_LORA)),
        "w_uq": nrm(ks[5], (DEPTH, Q_LORA, MLA_HEADS * MLA_QK), Q_LORA ** -0.5),
        "w_ukv": nrm(ks[6], (DEPTH, KV_LORA, MLA_HEADS * (MLA_NOPE + MLA_V)), KV_LORA ** -0.5),
        "g_mla_q": gain(ks[7], (DEPTH, MLA_QK)),
        "g_mla_k": gain(ks[8], (DEPTH, MLA_QK)),
        "g_dsa_q": gain(ks[9], (DEPTH, DSA_HD)),
        "g_dsa_k": gain(ks[10], (DEPTH, DSA_HD)),
        "rel_bias": nrm(ks[11], (REL_BUCKETS, DSA_HEADS), 0.5),
        "w_o": nrm(ks[12], (DEPTH, MIX_WIDTH, D_MODEL), MIX_WIDTH ** -0.5),
        "g_ffn": gain(ks[13], (DEPTH, D_MODEL)),
        "w_pq": nrm(ks[14], (DEPTH, D_MODEL, PEER_HEADS * PEER_DKEY), D_MODEL ** -0.5),
        "sub_keys": nrm(ks[15], (DEPTH, PEER_HEADS, 2, PEER_NKEYS, PEER_DKEY // 2), (PEER_DKEY // 2) ** -0.5),
        "u_tab": nrm(ks[16], (DEPTH, PEER_EXPERTS, D_MODEL), D_MODEL ** -0.5),
        "v_tab": nrm(ks[17], (DEPTH, PEER_EXPERTS, D_MODEL), PEER_HEADS ** -0.5),
    }


def reference(x, g_attn, w_in, g_cq, g_ckv, w_uq, w_ukv, g_mla_q, g_mla_k, g_dsa_q, g_dsa_k,
              rel_bias, w_o, g_ffn, w_pq, sub_keys, u_tab, v_tab):
    s = x.shape[1]
    pos = jnp.arange(s, dtype=jnp.int32)
    for l in range(DEPTH):
        h = rmsnorm(x, g_attn[l])
        proj = h @ w_in[l]
        c_q, c_kv, k_rope, q_d, k_d, v_d, q_i, k_i, w_i = jnp.split(proj, IN_OFFSETS, axis=-1)
        y_a = mla_attention(c_q, c_kv, k_rope, g_cq[l], g_ckv[l], w_uq[l], w_ukv[l],
                            g_mla_q[l], g_mla_k[l], pos)
        y_b = dsa_attention(q_d, k_d, v_d, q_i, k_i, w_i, g_dsa_q[l], g_dsa_k[l], rel_bias, pos)
        x = x + jnp.concatenate([y_a, y_b], axis=-1) @ w_o[l]
        x = x + peer(rmsnorm(x, g_ffn[l]), w_pq[l], sub_keys[l], u_tab[l], v_tab[l])
    return x
```

```python
import functools
import math

import numpy as np
import jax
import jax.numpy as jnp
from jax import lax
from jax.experimental import pallas as pl
from jax.experimental.pallas import tpu as pltpu

EPS = 1e-6
CHUNK = 64

MLA_HEADS = 8
MLA_NOPE = 128
MLA_ROPE = 64
MLA_V = 128
MLA_QK = MLA_NOPE + MLA_ROPE
MLA_QK_PAD = 256
Q_LORA = 512
KV_LORA = 256
ROPE_THETA = 10000.0

DSA_HEADS = 8
DSA_HD = 128
IDX_HEADS = 16
IDX_HD = 64
DSA_TOPK = 256

REL_BUCKETS = 32
REL_MAX_DIST = 128

PEER_HEADS = 8
PEER_NKEYS = 128
PEER_TOPK = 16

LANES = 128
NEG = -1e30
VMEM_LIMIT = 56 * 1024 * 1024

COL_Q, COL_K, COL_V, COL_QI = 0, 1024, 2048, 3072
COL_CQ, COL_CKV, COL_KR, COL_KI, COL_WI = 4096, 4608, 4864, 4992, 5120
IN_PAD = 5376

ATT_T = 256

_NT = (((1,), (1,)), ((), ()))


def _cparams(sem):
    return pltpu.CompilerParams(dimension_semantics=sem, vmem_limit_bytes=VMEM_LIMIT)


def _rep(x, n):
    return x if n == 1 else jnp.concatenate([x] * n, axis=1)


def _inproj_kernel(x_ref, g_ref, w_ref, o_ref, h_scr):
    @pl.when(pl.program_id(1) == 0)
    def _():
        x = x_ref[...]
        ms = jnp.mean(x * x, axis=-1, keepdims=True)
        h_scr[...] = (x * lax.rsqrt(ms + EPS) * g_ref[...]).astype(jnp.bfloat16)

    o_ref[...] = jnp.dot(h_scr[...], w_ref[...], preferred_element_type=jnp.float32)


def _inproj(x2d, g, w_p, tm=512, tn=768):
    m, d = x2d.shape
    n = w_p.shape[1]
    return pl.pallas_call(
        _inproj_kernel,
        grid=(m // tm, n // tn),
        in_specs=[pl.BlockSpec((tm, d), lambda i, j: (i, 0)),
                  pl.BlockSpec((1, d), lambda i, j: (0, 0)),
                  pl.BlockSpec((d, tn), lambda i, j: (0, j))],
        out_specs=pl.BlockSpec((tm, tn), lambda i, j: (i, j)),
        out_shape=jax.ShapeDtypeStruct((m, n), jnp.float32),
        scratch_shapes=[pltpu.VMEM((tm, d), jnp.bfloat16)],
        compiler_params=_cparams(("parallel", "arbitrary")),
        name="inproj",
    )(x2d, g, w_p)


def _rope128(xr, c, s1, s2):
    return xr * c + pltpu.roll(xr, 96, 1) * s2 + pltpu.roll(xr, 32, 1) * s1


def _mla_prep_kernel(cq_ref, ckv_ref, kr_ref, gcq_ref, gckv_ref, wuq_ref, wukv_ref, gq_ref, gk_ref,
                     c_ref, s1_ref, s2_ref, q_ref, k_ref, v_ref):
    cq = cq_ref[0]
    cqn = (cq * lax.rsqrt(jnp.mean(cq * cq, axis=-1, keepdims=True) + EPS) * gcq_ref[...])
    q_all = jnp.dot(cqn.astype(jnp.bfloat16), wuq_ref[...], preferred_element_type=jnp.float32)
    ckv = ckv_ref[0]
    ckvn = (ckv * lax.rsqrt(jnp.mean(ckv * ckv, axis=-1, keepdims=True) + EPS) * gckv_ref[...])
    kv = jnp.dot(ckvn.astype(jnp.bfloat16), wukv_ref[...], preferred_element_type=jnp.float32)
    kr = kr_ref[0]
    kr_ss = jnp.sum(kr * kr, axis=-1, keepdims=True)
    c, s1, s2 = c_ref[...], s1_ref[...], s2_ref[...]
    gq, gk = gq_ref[...], gk_ref[...]
    scale = MLA_QK ** -0.5
    for h in range(MLA_HEADS):
        qh = q_all[:, h * MLA_QK_PAD:(h + 1) * MLA_QK_PAD]
        rq = lax.rsqrt(jnp.sum(qh * qh, axis=-1, keepdims=True) * (1.0 / MLA_QK) + EPS)
        qn = qh * rq * gq
        q_ref[0, h, :, :MLA_NOPE] = (qn[:, :MLA_NOPE] * scale).astype(jnp.bfloat16)
        q_ref[0, h, :, MLA_NOPE:] = (_rope128(qn[:, MLA_NOPE:], c, s1, s2) * scale).astype(jnp.bfloat16)
        kn = kv[:, h * MLA_NOPE:(h + 1) * MLA_NOPE]
        rk = lax.rsqrt((jnp.sum(kn * kn, axis=-1, keepdims=True) + kr_ss) * (1.0 / MLA_QK) + EPS)
        k_ref[0, h, :, :MLA_NOPE] = (kn * rk * gk[:, :MLA_NOPE]).astype(jnp.bfloat16)
        k_ref[0, h, :, MLA_NOPE:] = _rope128(kr * rk * gk[:, MLA_NOPE:], c, s1, s2).astype(jnp.bfloat16)
        v_ref[0, h] = kv[:, MLA_HEADS * MLA_NOPE + h * MLA_V:MLA_HEADS * MLA_NOPE + (h + 1) * MLA_V
                         ].astype(jnp.bfloat16)


def _mla_prep(proj3, g_cq, g_ckv, wuq_p, wukv_p, gq_pad, gk_pad, rope_c, rope_s1, rope_s2, tm=256):
    b, s, _ = proj3.shape
    const = lambda shape: pl.BlockSpec(shape, lambda bi, i: (0,) * len(shape))
    return pl.pallas_call(
        _mla_prep_kernel,
        grid=(b, s // tm),
        in_specs=[pl.BlockSpec((1, tm, Q_LORA), lambda bi, i: (bi, i, COL_CQ // Q_LORA)),
                  pl.BlockSpec((1, tm, KV_LORA), lambda bi, i: (bi, i, COL_CKV // KV_LORA)),
                  pl.BlockSpec((1, tm, LANES), lambda bi, i: (bi, i, COL_KR // LANES)),
                  const((1, Q_LORA)), const((1, KV_LORA)),
                  const(wuq_p.shape), const(wukv_p.shape),
                  const((1, MLA_QK_PAD)), const((1, MLA_QK_PAD)),
                  pl.BlockSpec((tm, LANES), lambda bi, i: (i, 0)),
                  pl.BlockSpec((tm, LANES), lambda bi, i: (i, 0)),
                  pl.BlockSpec((tm, LANES), lambda bi, i: (i, 0))],
        out_specs=[pl.BlockSpec((1, MLA_HEADS, tm, MLA_QK_PAD), lambda bi, i: (bi, 0, i, 0)),
                   pl.BlockSpec((1, MLA_HEADS, tm, MLA_QK_PAD), lambda bi, i: (bi, 0, i, 0)),
                   pl.BlockSpec((1, MLA_HEADS, tm, MLA_V), lambda bi, i: (bi, 0, i, 0))],
        out_shape=[jax.ShapeDtypeStruct((b, MLA_HEADS, s, MLA_QK_PAD), jnp.bfloat16),
                   jax.ShapeDtypeStruct((b, MLA_HEADS, s, MLA_QK_PAD), jnp.bfloat16),
                   jax.ShapeDtypeStruct((b, MLA_HEADS, s, MLA_V), jnp.bfloat16)],
        compiler_params=_cparams(("parallel", "parallel")),
        name="mla_prep",
    )(proj3, proj3, proj3, g_cq, g_ckv, wuq_p, wukv_p, gq_pad, gk_pad, rope_c, rope_s1, rope_s2)


def _dsa_prep_kernel(q_ref, k_ref, v_ref, qi_ref, ki_ref, gq_ref, gk_ref,
                     qo_ref, ko_ref, vo_ref, qio_ref, kio_ref):
    gq, gk = gq_ref[...], gk_ref[...]
    scale = DSA_HD ** -0.5
    for h in range(DSA_HEADS):
        sl = slice(h * DSA_HD, (h + 1) * DSA_HD)
        qh = q_ref[0, :, sl]
        qo_ref[0, h] = (qh * lax.rsqrt(jnp.mean(qh * qh, axis=-1, keepdims=True) + EPS) * gq * scale
                        ).astype(jnp.bfloat16)
        kh = k_ref[0, :, sl]
        ko_ref[0, h] = (kh * lax.rsqrt(jnp.mean(kh * kh, axis=-1, keepdims=True) + EPS) * gk
                        ).astype(jnp.bfloat16)
        vo_ref[0, h] = v_ref[0, :, sl].astype(jnp.bfloat16)
    qio_ref[0] = (qi_ref[0] * (IDX_HD ** -0.5)).astype(jnp.bfloat16)
    ki = ki_ref[0]
    kio_ref[0, 0] = ki.astype(jnp.bfloat16)
    kio_ref[0, 1] = pltpu.roll(ki, IDX_HD, 1).astype(jnp.bfloat16)


def _dsa_prep(proj3, gq, gk, tm=256):
    b, s, _ = proj3.shape
    w = DSA_HEADS * DSA_HD
    hd_spec = pl.BlockSpec((1, DSA_HEADS, tm, DSA_HD), lambda bi, i: (bi, 0, i, 0))
    hd_shape = jax.ShapeDtypeStruct((b, DSA_HEADS, s, DSA_HD), jnp.bfloat16)
    return pl.pallas_call(
        _dsa_prep_kernel,
        grid=(b, s // tm),
        in_specs=[pl.BlockSpec((1, tm, w), lambda bi, i: (bi, i, COL_Q // w)),
                  pl.BlockSpec((1, tm, w), lambda bi, i: (bi, i, COL_K // w)),
                  pl.BlockSpec((1, tm, w), lambda bi, i: (bi, i, COL_V // w)),
                  pl.BlockSpec((1, tm, w), lambda bi, i: (bi, i, COL_QI // w)),
                  pl.BlockSpec((1, tm, LANES), lambda bi, i: (bi, i, COL_KI // LANES)),
                  pl.BlockSpec((1, DSA_HD), lambda bi, i: (0, 0)),
                  pl.BlockSpec((1, DSA_HD), lambda bi, i: (0, 0))],
        out_specs=[hd_spec, hd_spec, hd_spec,
                   pl.BlockSpec((1, tm, w), lambda bi, i: (bi, i, 0)),
                   pl.BlockSpec((1, 2, tm, LANES), lambda bi, i: (bi, 0, i, 0))],
        out_shape=[hd_shape, hd_shape, hd_shape,
                   jax.ShapeDtypeStruct((b, s, w), jnp.bfloat16),
                   jax.ShapeDtypeStruct((b, 2, s, LANES), jnp.bfloat16)],
        compiler_params=_cparams(("parallel", "parallel")),
        name="dsa_prep",
    )(proj3, proj3, proj3, proj3, proj3, gq, gk)


def _online_softmax_step(s, v, m_scr, l_scr, acc_scr):
    nrep = s.shape[1] // LANES
    m_prev = m_scr[...]
    m_next = jnp.maximum(m_prev, jnp.max(s, axis=1, keepdims=True))
    p = jnp.exp(s - _rep(m_next, nrep))
    alpha = jnp.exp(m_prev - m_next)
    l_scr[...] = alpha * l_scr[...] + jnp.sum(p, axis=1, keepdims=True)
    acc_scr[...] = alpha * acc_scr[...] + jnp.dot(p.astype(jnp.bfloat16), v,
                                                  preferred_element_type=jnp.float32)
    m_scr[...] = m_next


def _attn_init(m_scr, l_scr, acc_scr):
    m_scr[...] = jnp.full(m_scr.shape, NEG, jnp.float32)
    l_scr[...] = jnp.zeros(l_scr.shape, jnp.float32)
    acc_scr[...] = jnp.zeros(acc_scr.shape, jnp.float32)


def _mla_attn_kernel(q_ref, k_ref, v_ref, mask_ref, o_ref, m_scr, l_scr, acc_scr):
    qi = pl.program_id(2)
    t = q_ref.shape[2]
    _attn_init(m_scr, l_scr, acc_scr)
    q = q_ref[0, 0]

    def body(ki, carry):
        off = pl.multiple_of(ki * t, t)
        k = k_ref[0, 0, pl.ds(off, t), :]
        v = v_ref[0, 0, pl.ds(off, t), :]
        s = lax.dot_general(q, k, _NT, preferred_element_type=jnp.float32)
        s = s + mask_ref[(ki == qi).astype(jnp.int32)]
        _online_softmax_step(s, v, m_scr, l_scr, acc_scr)
        return carry

    lax.fori_loop(0, qi + 1, body, 0)
    o_ref[0] = (acc_scr[...] / l_scr[...]).astype(o_ref.dtype)


def _mla_attn(q, k, v, mask_tbl):
    b, h, s, dq = q.shape
    t = ATT_T
    return pl.pallas_call(
        _mla_attn_kernel,
        grid=(b, h, s // t),
        in_specs=[pl.BlockSpec((1, 1, t, dq), lambda bi, hi, i: (bi, hi, i, 0)),
                  pl.BlockSpec((1, 1, s, dq), lambda bi, hi, i: (bi, hi, 0, 0)),
                  pl.BlockSpec((1, 1, s, MLA_V), lambda bi, hi, i: (bi, hi, 0, 0)),
                  pl.BlockSpec((2, t, t), lambda bi, hi, i: (0, 0, 0))],
        out_specs=pl.BlockSpec((1, t, MLA_V), lambda bi, hi, i: (bi, i, hi)),
        out_shape=jax.ShapeDtypeStruct((b, s, h * MLA_V), jnp.bfloat16),
        scratch_shapes=[pltpu.VMEM((t, LANES), jnp.float32)] * 3,
        compiler_params=_cparams(("parallel", "parallel", "arbitrary")),
        name="mla_attn",
    )(q, k, v, mask_tbl)


def _dsa_attn_kernel(q_ref, k_ref, v_ref, sel_ref, bias_ref, o_ref, m_scr, l_scr, acc_scr):
    qi = pl.program_id(2)
    t = q_ref.shape[2]
    _attn_init(m_scr, l_scr, acc_scr)
    q = q_ref[0, 0]

    def body(ki, carry):
        off = pl.multiple_of(ki * t, t)
        k = k_ref[0, 0, pl.ds(off, t), :]
        v = v_ref[0, 0, pl.ds(off, t), :]
        s = lax.dot_general(q, k, _NT, preferred_element_type=jnp.float32)
        s = s + bias_ref[0, jnp.minimum(qi - ki, 2)] + sel_ref[0, 0, ki].astype(jnp.float32)
        _online_softmax_step(s, v, m_scr, l_scr, acc_scr)
        return carry

    lax.fori_loop(0, qi + 1, body, 0)
    o_ref[0] = (acc_scr[...] / l_scr[...]).astype(o_ref.dtype)


def _dsa_attn(q, k, v, sel, bias_tbl):
    b, h, s, d = q.shape
    t = ATT_T
    n = s // t
    return pl.pallas_call(
        _dsa_attn_kernel,
        grid=(b, h, n),
        in_specs=[pl.BlockSpec((1, 1, t, d), lambda bi, hi, i: (bi, hi, i, 0)),
                  pl.BlockSpec((1, 1, s, d), lambda bi, hi, i: (bi, hi, 0, 0)),
                  pl.BlockSpec((1, 1, s, d), lambda bi, hi, i: (bi, hi, 0, 0)),
                  pl.BlockSpec((1, 1, n, t, t), lambda bi, hi, i: (bi, i, 0, 0, 0)),
                  pl.BlockSpec((1, 3, t, t), lambda bi, hi, i: (hi, 0, 0, 0))],
        out_specs=pl.BlockSpec((1, t, d), lambda bi, hi, i: (bi, i, hi)),
        out_shape=jax.ShapeDtypeStruct((b, s, h * d), jnp.bfloat16),
        scratch_shapes=[pltpu.VMEM((t, LANES), jnp.float32)] * 3,
        compiler_params=_cparams(("parallel", "parallel", "arbitrary")),
        name="dsa_attn",
    )(q, k, v, sel, bias_tbl)


def _bias_kernel(rb_ref, o_ref):
    h = pl.program_id(0)
    t = o_ref.shape[2]
    row = lax.broadcasted_iota(jnp.int32, (t, t), 0)
    col = lax.broadcasted_iota(jnp.int32, (t, t), 1)
    nb = REL_BUCKETS // 2
    max_exact = nb // 2
    for r in range(2):
        rel = col - row - r * t
        n = jnp.abs(rel)
        n2 = n * n
        large = jnp.full((t, t), max_exact, jnp.int32)
        for j in range(1, nb - max_exact):
            large = large + (n2 >= (max_exact * max_exact) * (2 ** j)).astype(jnp.int32)
        bucket = jnp.where(rel > 0, nb, 0) + jnp.where(n < max_exact, n, large)
        val = jnp.zeros((t, t), jnp.float32)
        for bkt in range(REL_BUCKETS):
            val = jnp.where(bucket == bkt, rb_ref[bkt, h], val)
        o_ref[0, r] = val
    o_ref[0, 2] = jnp.full((t, t), rb_ref[nb - 1, h], jnp.float32)


def _bias_tiles(rel_bias):
    t = ATT_T
    assert t >= REL_MAX_DIST
    return pl.pallas_call(
        _bias_kernel,
        grid=(DSA_HEADS,),
        in_specs=[pl.BlockSpec(memory_space=pltpu.SMEM)],
        out_specs=pl.BlockSpec((1, 3, t, t), lambda h: (h, 0, 0, 0)),
        out_shape=jax.ShapeDtypeStruct((DSA_HEADS, 3, t, t), jnp.float32),
        compiler_params=_cparams(("arbitrary",)),
        name="t5_bias",
    )(rel_bias)


_KEY_NEG_INF = int(np.int32(np.uint32(0xFF800000) ^ np.uint32(0x7FFFFFFF)))
_INT_MIN = -2 ** 31


def _index_kernel(n_sel, q_ref, w_ref, k_ref, o_ref, key_scr, wb_scr):
    qi = pl.program_id(1)
    t = q_ref.shape[1]
    n_tiles = key_scr.shape[0]
    n_adm = qi + 1
    w = w_ref[0]
    for h in range(IDX_HEADS):
        wb_scr[h] = jnp.broadcast_to(w[:, h:h + 1], (t, LANES))
    row_chunk = (qi * t + lax.broadcasted_iota(jnp.int32, (t, 1), 0)) // CHUNK
    nrep = t // LANES

    def score_body(c, carry):
        off = pl.multiple_of(c * t, t)
        acc = jnp.zeros((t, t), jnp.float32)
        for hp in range(IDX_HEADS // 2):
            qpair = q_ref[0, :, hp * LANES:(hp + 1) * LANES]
            for par in range(2):
                kk = k_ref[0, par, pl.ds(off, t), :]
                d = lax.dot_general(qpair, kk, _NT, preferred_element_type=jnp.float32)
                acc = acc + jnp.maximum(d, 0.0) * _rep(wb_scr[2 * hp + par], nrep)
        col_chunk = (c * t + lax.broadcasted_iota(jnp.int32, (1, t), 1)) // CHUNK
        score = jnp.where(col_chunk <= row_chunk, acc, -jnp.inf)
        bits = pltpu.bitcast(score, jnp.int32)
        key_scr[c] = bits ^ ((bits >> 31) & 0x7FFFFFFF)
        return carry

    lax.fori_loop(0, n_adm, score_body, 0)

    def count_ge(cand):
        candb = jnp.broadcast_to(cand, (t, t))

        def cnt_body(c, cnt):
            ge = jnp.where(key_scr[c] >= candb, 1.0, 0.0)
            for r in range(nrep):
                cnt = cnt + ge[:, r * LANES:(r + 1) * LANES]
            return cnt

        cnt = lax.fori_loop(0, n_adm, cnt_body, jnp.zeros((t, LANES), jnp.float32))
        return jnp.sum(cnt, axis=1, keepdims=True)

    kf = float(n_sel)
    zero = jnp.zeros((t, 1), jnp.int32)
    thr = jnp.where(count_ge(zero) >= kf, zero, jnp.full((t, 1), _INT_MIN, jnp.int32))

    def bit_body(it, thr):
        cand = thr + jnp.left_shift(jnp.int32(1), 30 - it)
        return jnp.where(count_ge(cand) >= kf, cand, thr)

    thr = lax.fori_loop(0, 31, bit_body, thr)
    thrb = jnp.broadcast_to(thr, (t, t))

    def out_body(c, carry):
        key = key_scr[c]
        sel = jnp.logical_and(key >= thrb, key > _KEY_NEG_INF)
        o_ref[0, 0, c] = jnp.where(sel, 0.0, NEG).astype(o_ref.dtype)
        return carry

    lax.fori_loop(0, n_adm, out_body, 0)

    def fill_body(c, carry):
        o_ref[0, 0, c] = jnp.full((t, t), NEG, o_ref.dtype)
        return carry

    lax.fori_loop(n_adm, n_tiles, fill_body, 0)


def _dsa_index(q_idx, w3, k_idx2, n_sel, w_col_block):
    b, s, _ = q_idx.shape
    t = ATT_T
    n = s // t
    return pl.pallas_call(
        functools.partial(_index_kernel, n_sel),
        grid=(b, n),
        in_specs=[pl.BlockSpec((1, t, IDX_HEADS * IDX_HD), lambda bi, i: (bi, i, 0)),
                  pl.BlockSpec((1, t, LANES), lambda bi, i: (bi, i, w_col_block)),
                  pl.BlockSpec((1, 2, s, LANES), lambda bi, i: (bi, 0, 0, 0))],
        out_specs=pl.BlockSpec((1, 1, n, t, t), lambda bi, i: (bi, i, 0, 0, 0)),
        out_shape=jax.ShapeDtypeStruct((b, n, n, t, t), jnp.bfloat16),
        scratch_shapes=[pltpu.VMEM((n, t, t), jnp.int32),
                        pltpu.VMEM((IDX_HEADS, t, LANES), jnp.float32)],
        compiler_params=_cparams(("parallel", "parallel")),
        name="dsa_index",
    )(q_idx, w3, k_idx2)


def _outproj_kernel(x_ref, ya_ref, yb_ref, wa_ref, wb_ref, g_ref, x1_ref, h2_ref):
    y = (jnp.dot(ya_ref[...], wa_ref[...], preferred_element_type=jnp.float32)
         + jnp.dot(yb_ref[...], wb_ref[...], preferred_element_type=jnp.float32))
    x1 = x_ref[...] + y
    x1_ref[...] = x1
    ms = jnp.mean(x1 * x1, axis=-1, keepdims=True)
    h2_ref[...] = (x1 * lax.rsqrt(ms + EPS) * g_ref[...]).astype(jnp.bfloat16)


def _outproj(x2d, ya, yb, wo_a, wo_b, g_ffn, tm=256):
    m, d = x2d.shape
    ka, kb = ya.shape[1], yb.shape[1]
    return pl.pallas_call(
        _outproj_kernel,
        grid=(m // tm,),
        in_specs=[pl.BlockSpec((tm, d), lambda i: (i, 0)),
                  pl.BlockSpec((tm, ka), lambda i: (i, 0)),
                  pl.BlockSpec((tm, kb), lambda i: (i, 0)),
                  pl.BlockSpec((ka, d), lambda i: (0, 0)),
                  pl.BlockSpec((kb, d), lambda i: (0, 0)),
                  pl.BlockSpec((1, d), lambda i: (0, 0))],
        out_specs=[pl.BlockSpec((tm, d), lambda i: (i, 0)),
                   pl.BlockSpec((tm, d), lambda i: (i, 0))],
        out_shape=[jax.ShapeDtypeStruct((m, d), jnp.float32),
                   jax.ShapeDtypeStruct((m, d), jnp.bfloat16)],
        compiler_params=_cparams(("parallel",)),
        name="outproj",
    )(x2d, ya, yb, wo_a, wo_b, g_ffn)


def _top_values(cur, k):
    out = []
    for _ in range(k):
        m = jnp.max(cur, axis=0, keepdims=True)
        out.append(m)
        cur = jnp.where(cur == m, -jnp.inf, cur)
    return out


_N_TOP = PEER_TOPK + 1
_CAND_PAIRS = [(a, b) for a in range(_N_TOP) for b in range(_N_TOP) if (a + 1) * (b + 1) <= _N_TOP]


def _peer_score_kernel(h_ref, wq_ref, sk_ref, s1_ref, th_ref, a_ref):
    tm = h_ref.shape[0]
    qr_t = lax.dot_general(wq_ref[...], h_ref[...], _NT, preferred_element_type=jnp.float32)
    for h in range(PEER_HEADS):
        q0 = qr_t[(2 * h) * PEER_NKEYS:(2 * h + 1) * PEER_NKEYS].astype(jnp.bfloat16)
        q1 = qr_t[(2 * h + 1) * PEER_NKEYS:(2 * h + 2) * PEER_NKEYS].astype(jnp.bfloat16)
        s0 = jnp.dot(sk_ref[2 * h], q0, preferred_element_type=jnp.float32)
        s1 = jnp.dot(sk_ref[2 * h + 1], q1, preferred_element_type=jnp.float32)
        top0 = _top_values(s0, _N_TOP)
        top1 = _top_values(s1, _N_TOP)
        cands = [top0[a] + top1[b] for (a, b) in _CAND_PAIRS]
        pad = (-len(cands)) % 8
        cands += [jnp.full((1, tm), -jnp.inf, jnp.float32)] * pad
        vals = _top_values(jnp.concatenate(cands, axis=0), _N_TOP)
        z = jnp.zeros((1, tm), jnp.float32)
        for v in vals[:PEER_TOPK]:
            z = z + jnp.exp(v - vals[0])
        cut = 0.5 * (vals[PEER_TOPK - 1] + vals[PEER_TOPK])
        s1_ref[h] = s1 - top1[0]
        th_ref[h] = (cut - top1[0]) - s0
        a_ref[h] = jnp.exp(s0 - top0[0]) / z


def _peer_scores(h2, wpq_t, sk, tm=256):
    m, d = h2.shape
    nq = wpq_t.shape[0]
    out_spec = pl.BlockSpec((PEER_HEADS, PEER_NKEYS, tm), lambda i: (0, 0, i))
    out_shape = jax.ShapeDtypeStruct((PEER_HEADS, PEER_NKEYS, m), jnp.float32)
    return pl.pallas_call(
        _peer_score_kernel,
        grid=(m // tm,),
        in_specs=[pl.BlockSpec((tm, d), lambda i: (i, 0)),
                  pl.BlockSpec((nq, d), lambda i: (0, 0)),
                  pl.BlockSpec(sk.shape, lambda i: (0, 0, 0))],
        out_specs=[out_spec, out_spec, out_spec],
        out_shape=[out_shape, out_shape, out_shape],
        compiler_params=_cparams(("parallel",)),
        name="peer_scores",
    )(h2, wpq_t, sk)


def _gelu_exact(x):
    return 0.5 * x * (1.0 + lax.erf(x * (1.0 / math.sqrt(2.0))))


def _peer_main_kernel(h_ref, u_ref, vt_ref, s1_ref, th_ref, a_ref, x1_ref, o_ref,
                      acc_scr, b_scr, w_scr):
    e = pl.program_id(1)
    ni = th_ref.shape[1]

    @pl.when(e == 0)
    def _():
        acc_scr[...] = jnp.zeros(acc_scr.shape, jnp.float32)
        b_scr[...] = jnp.exp(s1_ref[...])

    act = lax.dot_general(u_ref[...], h_ref[...], _NT, preferred_element_type=jnp.float32)
    for il in range(ni):
        g = None
        for h in range(PEER_HEADS):
            th = th_ref[h, il:il + 1, :]
            a = a_ref[h, il:il + 1, :]
            term = a * jnp.where(s1_ref[h] >= th, b_scr[h], 0.0)
            g = term if g is None else g + term
        rows = slice(il * PEER_NKEYS, (il + 1) * PEER_NKEYS)
        w_scr[rows, :] = (g * _gelu_exact(act[rows, :])).astype(jnp.bfloat16)
    acc_scr[...] += jnp.dot(vt_ref[...], w_scr[...], preferred_element_type=jnp.float32)

    @pl.when(e == pl.num_programs(1) - 1)
    def _():
        o_ref[...] = x1_ref[...] + acc_scr[...].T


def _peer_main(h2, u_bf, vt_bf, s1s, ths, a_s, x1, tm=256, te=1024):
    m, d = h2.shape
    ne = u_bf.shape[0]
    ni = te // PEER_NKEYS
    return pl.pallas_call(
        _peer_main_kernel,
        grid=(m // tm, ne // te),
        in_specs=[pl.BlockSpec((tm, d), lambda i, e: (i, 0)),
                  pl.BlockSpec((te, d), lambda i, e: (e, 0)),
                  pl.BlockSpec((d, te), lambda i, e: (0, e)),
                  pl.BlockSpec((PEER_HEADS, PEER_NKEYS, tm), lambda i, e: (0, 0, i)),
                  pl.BlockSpec((PEER_HEADS, ni, tm), lambda i, e: (0, e, i)),
                  pl.BlockSpec((PEER_HEADS, ni, tm), lambda i, e: (0, e, i)),
                  pl.BlockSpec((tm, d), lambda i, e: (i, 0))],
        out_specs=pl.BlockSpec((tm, d), lambda i, e: (i, 0)),
        out_shape=jax.ShapeDtypeStruct((m, d), jnp.float32),
        scratch_shapes=[pltpu.VMEM((d, tm), jnp.float32),
                        pltpu.VMEM((PEER_HEADS, PEER_NKEYS, tm), jnp.float32),
                        pltpu.VMEM((te, tm), jnp.bfloat16)],
        compiler_params=_cparams(("parallel", "arbitrary")),
        name="peer_main",
    )(h2, u_bf, vt_bf, s1s, ths, a_s, x1)


def _rope_tables(s):
    half = MLA_ROPE // 2
    freqs = ROPE_THETA ** (-2.0 * jnp.arange(half, dtype=jnp.float32) / MLA_ROPE)
    ang = jnp.arange(s, dtype=jnp.float32)[:, None] * freqs[None, :]
    cos, sin = jnp.cos(ang), jnp.sin(ang)
    z = jnp.zeros_like(cos)
    c = jnp.concatenate([cos, cos, z, z], axis=1)
    s1 = jnp.concatenate([z, sin, z, z], axis=1)
    s2 = jnp.concatenate([-sin, z, z, z], axis=1)
    return c, s1, s2


def _mla_mask_table():
    t = ATT_T
    r = np.arange(t)[:, None] // CHUNK
    c = np.arange(t)[None, :] // CHUNK
    diag = np.where(c <= r, 0.0, NEG).astype(np.float32)
    return jnp.asarray(np.stack([np.zeros((t, t), np.float32), diag]))


def _layer(x, g_attn, w_in, g_cq, g_ckv, w_uq, w_ukv, g_mla_q, g_mla_k, g_dsa_q, g_dsa_k,
           rel_bias, w_o, g_ffn, w_pq, sub_keys, u_tab, v_tab):
    b, s, d = x.shape
    m = b * s
    bf = jnp.bfloat16
    f32 = jnp.float32
    assert s % ATT_T == 0 and m % 512 == 0

    o = np.cumsum([0, Q_LORA, KV_LORA, MLA_ROPE, 1024, 1024, 1024, 1024, IDX_HD, IDX_HEADS])
    zc = lambda n: jnp.zeros((d, n), f32)
    w_in_p = jnp.concatenate([
        w_in[:, o[3]:o[7]], w_in[:, o[0]:o[2]],
        w_in[:, o[2]:o[3]], zc(LANES - MLA_ROPE),
        w_in[:, o[7]:o[8]], zc(LANES - IDX_HD),
        w_in[:, o[8]:o[9]], zc(LANES - IDX_HEADS), zc(IN_PAD - COL_WI - LANES)], axis=1).astype(bf)
    wuq_p = jnp.pad(w_uq.reshape(Q_LORA, MLA_HEADS, MLA_QK),
                    ((0, 0), (0, 0), (0, MLA_QK_PAD - MLA_QK))).reshape(Q_LORA, -1).astype(bf)
    wukv_p = w_ukv.reshape(KV_LORA, MLA_HEADS, 2, MLA_NOPE).transpose(0, 2, 1, 3).reshape(KV_LORA, -1).astype(bf)
    gq_pad = jnp.pad(g_mla_q, (0, MLA_QK_PAD - MLA_QK))[None]
    gk_pad = jnp.pad(g_mla_k, (0, MLA_QK_PAD - MLA_QK))[None]
    rope_c, rope_s1, rope_s2 = _rope_tables(s)

    proj = _inproj(x.reshape(m, d), g_attn[None], w_in_p)
    proj3 = proj.reshape(b, s, IN_PAD)

    q_a, k_a, v_a = _mla_prep(proj3, g_cq[None], g_ckv[None], wuq_p, wukv_p, gq_pad, gk_pad,
                              rope_c, rope_s1, rope_s2)
    y_a = _mla_attn(q_a, k_a, v_a, _mla_mask_table())

    q_d, k_d, v_d, q_i, k_i2 = _dsa_prep(proj3, g_dsa_q[None], g_dsa_k[None])
    n_sel = min(DSA_TOPK, s // 4)
    sel = _dsa_index(q_i, proj3, k_i2, n_sel, COL_WI // LANES)
    y_b = _dsa_attn(q_d, k_d, v_d, sel, _bias_tiles(rel_bias))

    na = MLA_HEADS * MLA_V
    x1, h2 = _outproj(x.reshape(m, d), y_a.reshape(m, na), y_b.reshape(m, -1),
                      w_o[:na].astype(bf), w_o[na:].astype(bf), g_ffn[None])

    sk = sub_keys.reshape(PEER_HEADS * 2, PEER_NKEYS, -1).astype(bf)
    s1s, ths, a_s = _peer_scores(h2, w_pq.T.astype(bf), sk)
    x2 = _peer_main(h2, u_tab.astype(bf), v_tab.T.astype(bf), s1s, ths, a_s, x1)
    return x2.reshape(b, s, d)


def kernel(x, g_attn, w_in, g_cq, g_ckv, w_uq, w_ukv, g_mla_q, g_mla_k, g_dsa_q, g_dsa_k,
           rel_bias, w_o, g_ffn, w_pq, sub_keys, u_tab, v_tab):
    for l in range(w_in.shape[0]):
        x = _layer(x, g_attn[l], w_in[l], g_cq[l], g_ckv[l], w_uq[l], w_ukv[l], g_mla_q[l], g_mla_k[l],
                   g_dsa_q[l], g_dsa_k[l], rel_bias, w_o[l], g_ffn[l], w_pq[l], sub_keys[l],
                   u_tab[l], v_tab[l])
    return x
```

```python
import functools
import math

import numpy as np
import jax
import jax.numpy as jnp
from jax import lax
from jax.experimental import pallas as pl
from jax.experimental.pallas import tpu as pltpu

EPS = 1e-6
CHUNK = 64

MLA_HEADS = 8
MLA_NOPE = 128
MLA_ROPE = 64
MLA_V = 128
MLA_QK = MLA_NOPE + MLA_ROPE
MLA_QK_PAD = 256
Q_LORA = 512
KV_LORA = 256
ROPE_THETA = 10000.0

DSA_HEADS = 8
DSA_HD = 128
IDX_HEADS = 16
IDX_HD = 64
DSA_TOPK = 256

REL_BUCKETS = 32
REL_MAX_DIST = 128

PEER_HEADS = 8
PEER_NKEYS = 128
PEER_TOPK = 16

LANES = 128
NEG = -1e30
VMEM_LIMIT = 56 * 1024 * 1024

COL_Q, COL_K, COL_V, COL_QI = 0, 1024, 2048, 3072
COL_CQ, COL_CKV, COL_KR, COL_KI, COL_WI = 4096, 4608, 4864, 4992, 5120
IN_PAD = 5376

ATT_T = 256

_NT = (((1,), (1,)), ((), ()))


def _cparams(sem):
    return pltpu.CompilerParams(dimension_semantics=sem, vmem_limit_bytes=VMEM_LIMIT)


def _rep(x, n):
    return x if n == 1 else jnp.concatenate([x] * n, axis=1)


def _inproj_kernel(x_ref, g_ref, w_ref, o_ref, h_scr):
    @pl.when(pl.program_id(1) == 0)
    def _():
        x = x_ref[...]
        ms = jnp.mean(x * x, axis=-1, keepdims=True)
        h_scr[...] = (x * lax.rsqrt(ms + EPS) * g_ref[...]).astype(jnp.bfloat16)

    o_ref[...] = jnp.dot(h_scr[...], w_ref[...], preferred_element_type=jnp.float32)


def _inproj(x2d, g, w_p, tm=512, tn=768):
    m, d = x2d.shape
    n = w_p.shape[1]
    return pl.pallas_call(
        _inproj_kernel,
        grid=(m // tm, n // tn),
        in_specs=[pl.BlockSpec((tm, d), lambda i, j: (i, 0)),
                  pl.BlockSpec((1, d), lambda i, j: (0, 0)),
                  pl.BlockSpec((d, tn), lambda i, j: (0, j))],
        out_specs=pl.BlockSpec((tm, tn), lambda i, j: (i, j)),
        out_shape=jax.ShapeDtypeStruct((m, n), jnp.float32),
        scratch_shapes=[pltpu.VMEM((tm, d), jnp.bfloat16)],
        compiler_params=_cparams(("parallel", "arbitrary")),
        name="inproj",
    )(x2d, g, w_p)


def _rope128(xr, c, s1, s2):
    return xr * c + pltpu.roll(xr, 96, 1) * s2 + pltpu.roll(xr, 32, 1) * s1


def _mla_prep_kernel(cq_ref, ckv_ref, kr_ref, gcq_ref, gckv_ref, wuq_ref, wukv_ref, gq_ref, gk_ref,
                     c_ref, s1_ref, s2_ref, q_ref, k_ref, v_ref):
    cq = cq_ref[0]
    cqn = (cq * lax.rsqrt(jnp.mean(cq * cq, axis=-1, keepdims=True) + EPS) * gcq_ref[...])
    q_all = jnp.dot(cqn.astype(jnp.bfloat16), wuq_ref[...], preferred_element_type=jnp.float32)
    ckv = ckv_ref[0]
    ckvn = (ckv * lax.rsqrt(jnp.mean(ckv * ckv, axis=-1, keepdims=True) + EPS) * gckv_ref[...])
    kv = jnp.dot(ckvn.astype(jnp.bfloat16), wukv_ref[...], preferred_element_type=jnp.float32)
    kr = kr_ref[0]
    kr_ss = jnp.sum(kr * kr, axis=-1, keepdims=True)
    c, s1, s2 = c_ref[...], s1_ref[...], s2_ref[...]
    gq, gk = gq_ref[...], gk_ref[...]
    scale = MLA_QK ** -0.5
    for h in range(MLA_HEADS):
        qh = q_all[:, h * MLA_QK_PAD:(h + 1) * MLA_QK_PAD]
        rq = lax.rsqrt(jnp.sum(qh * qh, axis=-1, keepdims=True) * (1.0 / MLA_QK) + EPS)
        qn = qh * rq * gq
        q_ref[0, h, :, :MLA_NOPE] = (qn[:, :MLA_NOPE] * scale).astype(jnp.bfloat16)
        q_ref[0, h, :, MLA_NOPE:] = (_rope128(qn[:, MLA_NOPE:], c, s1, s2) * scale).astype(jnp.bfloat16)
        kn = kv[:, h * MLA_NOPE:(h + 1) * MLA_NOPE]
        rk = lax.rsqrt((jnp.sum(kn * kn, axis=-1, keepdims=True) + kr_ss) * (1.0 / MLA_QK) + EPS)
        k_ref[0, h, :, :MLA_NOPE] = (kn * rk * gk[:, :MLA_NOPE]).astype(jnp.bfloat16)
        k_ref[0, h, :, MLA_NOPE:] = _rope128(kr * rk * gk[:, MLA_NOPE:], c, s1, s2).astype(jnp.bfloat16)
        v_ref[0, h] = kv[:, MLA_HEADS * MLA_NOPE + h * MLA_V:MLA_HEADS * MLA_NOPE + (h + 1) * MLA_V
                         ].astype(jnp.bfloat16)


def _mla_prep(proj3, g_cq, g_ckv, wuq_p, wukv_p, gq_pad, gk_pad, rope_c, rope_s1, rope_s2, tm=256):
    b, s, _ = proj3.shape
    const = lambda shape: pl.BlockSpec(shape, lambda bi, i: (0,) * len(shape))
    return pl.pallas_call(
        _mla_prep_kernel,
        grid=(b, s // tm),
        in_specs=[pl.BlockSpec((1, tm, Q_LORA), lambda bi, i: (bi, i, COL_CQ // Q_LORA)),
                  pl.BlockSpec((1, tm, KV_LORA), lambda bi, i: (bi, i, COL_CKV // KV_LORA)),
                  pl.BlockSpec((1, tm, LANES), lambda bi, i: (bi, i, COL_KR // LANES)),
                  const((1, Q_LORA)), const((1, KV_LORA)),
                  const(wuq_p.shape), const(wukv_p.shape),
                  const((1, MLA_QK_PAD)), const((1, MLA_QK_PAD)),
                  pl.BlockSpec((tm, LANES), lambda bi, i: (i, 0)),
                  pl.BlockSpec((tm, LANES), lambda bi, i: (i, 0)),
                  pl.BlockSpec((tm, LANES), lambda bi, i: (i, 0))],
        out_specs=[pl.BlockSpec((1, MLA_HEADS, tm, MLA_QK_PAD), lambda bi, i: (bi, 0, i, 0)),
                   pl.BlockSpec((1, MLA_HEADS, tm, MLA_QK_PAD), lambda bi, i: (bi, 0, i, 0)),
                   pl.BlockSpec((1, MLA_HEADS, tm, MLA_V), lambda bi, i: (bi, 0, i, 0))],
        out_shape=[jax.ShapeDtypeStruct((b, MLA_HEADS, s, MLA_QK_PAD), jnp.bfloat16),
                   jax.ShapeDtypeStruct((b, MLA_HEADS, s, MLA_QK_PAD), jnp.bfloat16),
                   jax.ShapeDtypeStruct((b, MLA_HEADS, s, MLA_V), jnp.bfloat16)],
        compiler_params=_cparams(("parallel", "parallel")),
        name="mla_prep",
    )(proj3, proj3, proj3, g_cq, g_ckv, wuq_p, wukv_p, gq_pad, gk_pad, rope_c, rope_s1, rope_s2)


def _dsa_prep_kernel(q_ref, k_ref, v_ref, qi_ref, ki_ref, gq_ref, gk_ref,
                     qo_ref, ko_ref, vo_ref, qio_ref, kio_ref):
    gq, gk = gq_ref[...], gk_ref[...]
    scale = DSA_HD ** -0.5
    for h in range(DSA_HEADS):
        sl = slice(h * DSA_HD, (h + 1) * DSA_HD)
        qh = q_ref[0, :, sl]
        qo_ref[0, h] = (qh * lax.rsqrt(jnp.mean(qh * qh, axis=-1, keepdims=True) + EPS) * gq * scale
                        ).astype(jnp.bfloat16)
        kh = k_ref[0, :, sl]
        ko_ref[0, h] = (kh * lax.rsqrt(jnp.mean(kh * kh, axis=-1, keepdims=True) + EPS) * gk
                        ).astype(jnp.bfloat16)
        vo_ref[0, h] = v_ref[0, :, sl].astype(jnp.bfloat16)
    qio_ref[0] = (qi_ref[0] * (IDX_HD ** -0.5)).astype(jnp.bfloat16)
    ki = ki_ref[0]
    kio_ref[0, 0] = ki.astype(jnp.bfloat16)
    kio_ref[0, 1] = pltpu.roll(ki, IDX_HD, 1).astype(jnp.bfloat16)


def _dsa_prep(proj3, gq, gk, tm=256):
    b, s, _ = proj3.shape
    w = DSA_HEADS * DSA_HD
    hd_spec = pl.BlockSpec((1, DSA_HEADS, tm, DSA_HD), lambda bi, i: (bi, 0, i, 0))
    hd_shape = jax.ShapeDtypeStruct((b, DSA_HEADS, s, DSA_HD), jnp.bfloat16)
    return pl.pallas_call(
        _dsa_prep_kernel,
        grid=(b, s // tm),
        in_specs=[pl.BlockSpec((1, tm, w), lambda bi, i: (bi, i, COL_Q // w)),
                  pl.BlockSpec((1, tm, w), lambda bi, i: (bi, i, COL_K // w)),
                  pl.BlockSpec((1, tm, w), lambda bi, i: (bi, i, COL_V // w)),
                  pl.BlockSpec((1, tm, w), lambda bi, i: (bi, i, COL_QI // w)),
                  pl.BlockSpec((1, tm, LANES), lambda bi, i: (bi, i, COL_KI // LANES)),
                  pl.BlockSpec((1, DSA_HD), lambda bi, i: (0, 0)),
                  pl.BlockSpec((1, DSA_HD), lambda bi, i: (0, 0))],
        out_specs=[hd_spec, hd_spec, hd_spec,
                   pl.BlockSpec((1, tm, w), lambda bi, i: (bi, i, 0)),
                   pl.BlockSpec((1, 2, tm, LANES), lambda bi, i: (bi, 0, i, 0))],
        out_shape=[hd_shape, hd_shape, hd_shape,
                   jax.ShapeDtypeStruct((b, s, w), jnp.bfloat16),
                   jax.ShapeDtypeStruct((b, 2, s, LANES), jnp.bfloat16)],
        compiler_params=_cparams(("parallel", "parallel")),
        name="dsa_prep",
    )(proj3, proj3, proj3, proj3, proj3, gq, gk)


def _lane_fold(x, op):
    out = x[:, :LANES]
    for c in range(1, x.shape[1] // LANES):
        out = op(out, x[:, c * LANES:(c + 1) * LANES])
    return out


def _attn_sweep(q_ref, k_ref, v_ref, o_ref, m_scr, l_scr, acc_scr, add_fn):
    qi = pl.program_id(2)
    t = ATT_T
    nsub = q_ref.shape[2] // t
    nrep = t // LANES
    first = nsub * qi

    def scores(r, ki, rel):
        off = pl.multiple_of(ki * t, t)
        k = k_ref[0, 0, pl.ds(off, t), :]
        q = q_ref[0, 0, r * t:(r + 1) * t, :]
        s = lax.dot_general(q, k, _NT, preferred_element_type=jnp.float32)
        a = add_fn(r, ki, rel)
        return s if a is None else s + a

    def sweep(tile_fn):
        def body(ki, carry):
            for r in range(nsub):
                tile_fn(r, ki, None)
            return carry

        lax.fori_loop(0, first, body, 0)
        for j in range(nsub):
            for r in range(j, nsub):
                tile_fn(r, first + j, j - r)

    m_scr[...] = jnp.full(m_scr.shape, NEG, jnp.float32)

    def max_tile(r, ki, rel):
        m_scr[r] = jnp.maximum(m_scr[r], _lane_fold(scores(r, ki, rel), jnp.maximum))

    sweep(max_tile)
    for r in range(nsub):
        m_scr[r] = jnp.broadcast_to(jnp.max(m_scr[r], axis=1, keepdims=True), (t, LANES))
    l_scr[...] = jnp.zeros(l_scr.shape, jnp.float32)
    acc_scr[...] = jnp.zeros(acc_scr.shape, jnp.float32)

    def acc_tile(r, ki, rel):
        p = jnp.exp(scores(r, ki, rel) - _rep(m_scr[r], nrep))
        l_scr[r] += _lane_fold(p, jnp.add)
        off = pl.multiple_of(ki * t, t)
        acc_scr[r] += jnp.dot(p.astype(jnp.bfloat16), v_ref[0, 0, pl.ds(off, t), :],
                              preferred_element_type=jnp.float32)

    sweep(acc_tile)
    for r in range(nsub):
        l = jnp.sum(l_scr[r], axis=1, keepdims=True)
        o_ref[0, r * t:(r + 1) * t, :] = (acc_scr[r] / l).astype(o_ref.dtype)


def _attn_nsub(n_tiles):
    return 4 if n_tiles % 4 == 0 else (2 if n_tiles % 2 == 0 else 1)


def _mla_attn_kernel(q_ref, k_ref, v_ref, mask_ref, o_ref, m_scr, l_scr, acc_scr):
    def add_fn(r, ki, rel):
        return mask_ref[...] if rel == 0 else None

    _attn_sweep(q_ref, k_ref, v_ref, o_ref, m_scr, l_scr, acc_scr, add_fn)


def _mla_attn(q, k, v, mask_tbl):
    b, h, s, dq = q.shape
    t = ATT_T
    nsub = _attn_nsub(s // t)
    tq = nsub * t
    return pl.pallas_call(
        _mla_attn_kernel,
        grid=(b, h, s // tq),
        in_specs=[pl.BlockSpec((1, 1, tq, dq), lambda bi, hi, i: (bi, hi, i, 0)),
                  pl.BlockSpec((1, 1, s, dq), lambda bi, hi, i: (bi, hi, 0, 0)),
                  pl.BlockSpec((1, 1, s, MLA_V), lambda bi, hi, i: (bi, hi, 0, 0)),
                  pl.BlockSpec((t, t), lambda bi, hi, i: (0, 0))],
        out_specs=pl.BlockSpec((1, tq, MLA_V), lambda bi, hi, i: (bi, i, hi)),
        out_shape=jax.ShapeDtypeStruct((b, s, h * MLA_V), jnp.bfloat16),
        scratch_shapes=[pltpu.VMEM((nsub, t, LANES), jnp.float32)] * 3,
        compiler_params=_cparams(("parallel", "parallel", "arbitrary")),
        name="mla_attn",
    )(q, k, v, mask_tbl)


def _dsa_attn_kernel(q_ref, k_ref, v_ref, sel_ref, bias_ref, o_ref, m_scr, l_scr, acc_scr):
    nsub = q_ref.shape[2] // ATT_T
    first = nsub * pl.program_id(2)

    def add_fn(r, ki, rel):
        back = jnp.minimum(first + r - ki, 2) if rel is None else min(-rel, 2)
        return bias_ref[0, back] + sel_ref[0, r, ki].astype(jnp.float32)

    _attn_sweep(q_ref, k_ref, v_ref, o_ref, m_scr, l_scr, acc_scr, add_fn)


def _dsa_attn(q, k, v, sel, bias_tbl):
    b, h, s, d = q.shape
    t = ATT_T
    n = s // t
    nsub = _attn_nsub(n)
    tq = nsub * t
    return pl.pallas_call(
        _dsa_attn_kernel,
        grid=(b, h, s // tq),
        in_specs=[pl.BlockSpec((1, 1, tq, d), lambda bi, hi, i: (bi, hi, i, 0)),
                  pl.BlockSpec((1, 1, s, d), lambda bi, hi, i: (bi, hi, 0, 0)),
                  pl.BlockSpec((1, 1, s, d), lambda bi, hi, i: (bi, hi, 0, 0)),
                  pl.BlockSpec((1, nsub, n, t, t), lambda bi, hi, i: (bi, i, 0, 0, 0)),
                  pl.BlockSpec((1, 3, t, t), lambda bi, hi, i: (hi, 0, 0, 0))],
        out_specs=pl.BlockSpec((1, tq, d), lambda bi, hi, i: (bi, i, hi)),
        out_shape=jax.ShapeDtypeStruct((b, s, h * d), jnp.bfloat16),
        scratch_shapes=[pltpu.VMEM((nsub, t, LANES), jnp.float32)] * 3,
        compiler_params=_cparams(("parallel", "parallel", "arbitrary")),
        name="dsa_attn",
    )(q, k, v, sel, bias_tbl)


def _bias_kernel(rb_ref, o_ref):
    h = pl.program_id(0)
    t = o_ref.shape[2]
    row = lax.broadcasted_iota(jnp.int32, (t, t), 0)
    col = lax.broadcasted_iota(jnp.int32, (t, t), 1)
    nb = REL_BUCKETS // 2
    max_exact = nb // 2
    for r in range(2):
        rel = col - row - r * t
        n = jnp.abs(rel)
        n2 = n * n
        large = jnp.full((t, t), max_exact, jnp.int32)
        for j in range(1, nb - max_exact):
            large = large + (n2 >= (max_exact * max_exact) * (2 ** j)).astype(jnp.int32)
        bucket = jnp.where(rel > 0, nb, 0) + jnp.where(n < max_exact, n, large)
        val = jnp.zeros((t, t), jnp.float32)
        for bkt in range(REL_BUCKETS):
            val = jnp.where(bucket == bkt, rb_ref[bkt, h], val)
        o_ref[0, r] = val
    o_ref[0, 2] = jnp.full((t, t), rb_ref[nb - 1, h], jnp.float32)


def _bias_tiles(rel_bias):
    t = ATT_T
    assert t >= REL_MAX_DIST
    return pl.pallas_call(
        _bias_kernel,
        grid=(DSA_HEADS,),
        in_specs=[pl.BlockSpec(memory_space=pltpu.SMEM)],
        out_specs=pl.BlockSpec((1, 3, t, t), lambda h: (h, 0, 0, 0)),
        out_shape=jax.ShapeDtypeStruct((DSA_HEADS, 3, t, t), jnp.float32),
        compiler_params=_cparams(("arbitrary",)),
        name="t5_bias",
    )(rel_bias)


_KEY_NEG_INF = int(np.int32(np.uint32(0xFF800000) ^ np.uint32(0x7FFFFFFF)))
_INT_MIN = -2 ** 31


def _index_kernel(n_sel, q_ref, w_ref, k_ref, o_ref, key_scr, wb_scr):
    qi = pl.program_id(1)
    t = q_ref.shape[1]
    n_tiles = key_scr.shape[0]
    n_adm = qi + 1
    w = w_ref[0]
    for h in range(IDX_HEADS):
        wb_scr[h] = jnp.broadcast_to(w[:, h:h + 1], (t, LANES))
    row_chunk = (qi * t + lax.broadcasted_iota(jnp.int32, (t, 1), 0)) // CHUNK
    nrep = t // LANES

    def score_body(c, carry):
        off = pl.multiple_of(c * t, t)
        acc = jnp.zeros((t, t), jnp.float32)
        for hp in range(IDX_HEADS // 2):
            qpair = q_ref[0, :, hp * LANES:(hp + 1) * LANES]
            for par in range(2):
                kk = k_ref[0, par, pl.ds(off, t), :]
                d = lax.dot_general(qpair, kk, _NT, preferred_element_type=jnp.float32)
                acc = acc + jnp.maximum(d, 0.0) * _rep(wb_scr[2 * hp + par], nrep)
        col_chunk = (c * t + lax.broadcasted_iota(jnp.int32, (1, t), 1)) // CHUNK
        score = jnp.where(col_chunk <= row_chunk, acc, -jnp.inf)
        bits = pltpu.bitcast(score, jnp.int32)
        key_scr[c] = bits ^ ((bits >> 31) & 0x7FFFFFFF)
        return carry

    lax.fori_loop(0, n_adm, score_body, 0)

    def count_ge(cand):
        candb = jnp.broadcast_to(cand, (t, t))

        def cnt_body(c, cnt):
            ge = jnp.where(key_scr[c] >= candb, 1.0, 0.0)
            for r in range(nrep):
                cnt = cnt + ge[:, r * LANES:(r + 1) * LANES]
            return cnt

        cnt = lax.fori_loop(0, n_adm, cnt_body, jnp.zeros((t, LANES), jnp.float32))
        return jnp.sum(cnt, axis=1, keepdims=True)

    kf = float(n_sel)
    zero = jnp.zeros((t, 1), jnp.int32)
    thr = jnp.where(count_ge(zero) >= kf, zero, jnp.full((t, 1), _INT_MIN, jnp.int32))

    def bit_body(it, thr):
        cand = thr + jnp.left_shift(jnp.int32(1), 30 - it)
        return jnp.where(count_ge(cand) >= kf, cand, thr)

    thr = lax.fori_loop(0, 31, bit_body, thr)
    thrb = jnp.broadcast_to(thr, (t, t))

    def out_body(c, carry):
        key = key_scr[c]
        sel = jnp.logical_and(key >= thrb, key > _KEY_NEG_INF)
        o_ref[0, 0, c] = jnp.where(sel, 0.0, NEG).astype(o_ref.dtype)
        return carry

    lax.fori_loop(0, n_adm, out_body, 0)

    def fill_body(c, carry):
        o_ref[0, 0, c] = jnp.full((t, t), NEG, o_ref.dtype)
        return carry

    lax.fori_loop(n_adm, n_tiles, fill_body, 0)


def _dsa_index(q_idx, w3, k_idx2, n_sel, w_col_block):
    b, s, _ = q_idx.shape
    t = ATT_T
    n = s // t
    return pl.pallas_call(
        functools.partial(_index_kernel, n_sel),
        grid=(b, n),
        in_specs=[pl.BlockSpec((1, t, IDX_HEADS * IDX_HD), lambda bi, i: (bi, i, 0)),
                  pl.BlockSpec((1, t, LANES), lambda bi, i: (bi, i, w_col_block)),
                  pl.BlockSpec((1, 2, s, LANES), lambda bi, i: (bi, 0, 0, 0))],
        out_specs=pl.BlockSpec((1, 1, n, t, t), lambda bi, i: (bi, i, 0, 0, 0)),
        out_shape=jax.ShapeDtypeStruct((b, n, n, t, t), jnp.bfloat16),
        scratch_shapes=[pltpu.VMEM((n, t, t), jnp.int32),
                        pltpu.VMEM((IDX_HEADS, t, LANES), jnp.float32)],
        compiler_params=_cparams(("parallel", "parallel")),
        name="dsa_index",
    )(q_idx, w3, k_idx2)


def _outproj_kernel(x_ref, ya_ref, yb_ref, wa_ref, wb_ref, g_ref, x1_ref, h2_ref):
    y = (jnp.dot(ya_ref[...], wa_ref[...], preferred_element_type=jnp.float32)
         + jnp.dot(yb_ref[...], wb_ref[...], preferred_element_type=jnp.float32))
    x1 = x_ref[...] + y
    x1_ref[...] = x1
    ms = jnp.mean(x1 * x1, axis=-1, keepdims=True)
    h2_ref[...] = (x1 * lax.rsqrt(ms + EPS) * g_ref[...]).astype(jnp.bfloat16)


def _outproj(x2d, ya, yb, wo_a, wo_b, g_ffn, tm=256):
    m, d = x2d.shape
    ka, kb = ya.shape[1], yb.shape[1]
    return pl.pallas_call(
        _outproj_kernel,
        grid=(m // tm,),
        in_specs=[pl.BlockSpec((tm, d), lambda i: (i, 0)),
                  pl.BlockSpec((tm, ka), lambda i: (i, 0)),
                  pl.BlockSpec((tm, kb), lambda i: (i, 0)),
                  pl.BlockSpec((ka, d), lambda i: (0, 0)),
                  pl.BlockSpec((kb, d), lambda i: (0, 0)),
                  pl.BlockSpec((1, d), lambda i: (0, 0))],
        out_specs=[pl.BlockSpec((tm, d), lambda i: (i, 0)),
                   pl.BlockSpec((tm, d), lambda i: (i, 0))],
        out_shape=[jax.ShapeDtypeStruct((m, d), jnp.float32),
                   jax.ShapeDtypeStruct((m, d), jnp.bfloat16)],
        compiler_params=_cparams(("parallel",)),
        name="outproj",
    )(x2d, ya, yb, wo_a, wo_b, g_ffn)


def _top_values(cur, k):
    out = []
    for _ in range(k):
        m = jnp.max(cur, axis=0, keepdims=True)
        out.append(m)
        cur = jnp.where(cur == m, -jnp.inf, cur)
    return out


_N_TOP = PEER_TOPK + 1
_CAND_PAIRS = [(a, b) for a in range(_N_TOP) for b in range(_N_TOP) if (a + 1) * (b + 1) <= _N_TOP]


def _peer_score_kernel(h_ref, wq_ref, sk_ref, b_ref, beta_ref, a_ref):
    tm = h_ref.shape[0]
    qr_t = lax.dot_general(wq_ref[...], h_ref[...], _NT, preferred_element_type=jnp.float32)
    for h in range(PEER_HEADS):
        q0 = qr_t[(2 * h) * PEER_NKEYS:(2 * h + 1) * PEER_NKEYS].astype(jnp.bfloat16)
        q1 = qr_t[(2 * h + 1) * PEER_NKEYS:(2 * h + 2) * PEER_NKEYS].astype(jnp.bfloat16)
        s0 = jnp.dot(sk_ref[2 * h], q0, preferred_element_type=jnp.float32)
        s1 = jnp.dot(sk_ref[2 * h + 1], q1, preferred_element_type=jnp.float32)
        top0 = _top_values(s0, _N_TOP)
        top1 = _top_values(s1, _N_TOP)
        cands = [top0[a] + top1[b] for (a, b) in _CAND_PAIRS]
        pad = (-len(cands)) % 8
        cands += [jnp.full((1, tm), -jnp.inf, jnp.float32)] * pad
        vals = _top_values(jnp.concatenate(cands, axis=0), _N_TOP)
        z = jnp.zeros((1, tm), jnp.float32)
        for v in vals[:PEER_TOPK]:
            z = z + jnp.exp(v - vals[0])
        cut = 0.5 * (vals[PEER_TOPK - 1] + vals[PEER_TOPK])
        b_ref[h] = jnp.exp(s1 - top1[0])
        beta_ref[h] = jnp.exp((cut - top1[0]) - s0)
        a_ref[h] = jnp.exp(s0 - top0[0]) / z


def _peer_scores(h2, wpq_t, sk, tm=256):
    m, d = h2.shape
    nq = wpq_t.shape[0]
    out_spec = pl.BlockSpec((PEER_HEADS, PEER_NKEYS, tm), lambda i: (0, 0, i))
    out_shape = jax.ShapeDtypeStruct((PEER_HEADS, PEER_NKEYS, m), jnp.float32)
    return pl.pallas_call(
        _peer_score_kernel,
        grid=(m // tm,),
        in_specs=[pl.BlockSpec((tm, d), lambda i: (i, 0)),
                  pl.BlockSpec((nq, d), lambda i: (0, 0)),
                  pl.BlockSpec(sk.shape, lambda i: (0, 0, 0))],
        out_specs=[out_spec, out_spec, out_spec],
        out_shape=[out_shape, out_shape, out_shape],
        compiler_params=_cparams(("parallel",)),
        name="peer_scores",
    )(h2, wpq_t, sk)


def _gelu_exact(x):
    return 0.5 * x * (1.0 + lax.erf(x * (1.0 / math.sqrt(2.0))))


PEER_KC = 256


def _peer_main_step(h_ref, u_ref, vt_ref, b_ref, beta_ref, a_ref, acc_scr, w_scr, act_next, act_cur):
    tm = h_ref.shape[0]
    te = u_ref.shape[0]
    act_next[...] = lax.dot_general(u_ref[...], h_ref[...], _NT, preferred_element_type=jnp.float32)
    parts = []
    for c in range(te // PEER_KC):
        for il in range(c * PEER_KC // PEER_NKEYS, (c + 1) * PEER_KC // PEER_NKEYS):
            rows = slice(il * PEER_NKEYS, (il + 1) * PEER_NKEYS)
            for lt in range(tm // LANES):
                lanes = slice(lt * LANES, (lt + 1) * LANES)
                g = None
                for h in range(PEER_HEADS):
                    b = b_ref[h, :, lanes]
                    term = a_ref[h, il:il + 1, lanes] * jnp.where(b >= beta_ref[h, il:il + 1, lanes], b, 0.0)
                    g = term if g is None else g + term
                w_scr[rows, lanes] = (g * _gelu_exact(act_cur[rows, lanes])).astype(jnp.bfloat16)
        ck = slice(c * PEER_KC, (c + 1) * PEER_KC)
        parts.append(jnp.dot(vt_ref[:, ck], w_scr[ck, :], preferred_element_type=jnp.float32))
    acc_scr[...] += functools.reduce(lambda p, q: p + q, parts)


def _peer_main_kernel(h_ref, u_ref, vt_ref, b_ref, beta_ref, a_ref, x1_ref, o_ref,
                      acc_scr, w_scr, act_a, act_b):
    e = pl.program_id(1)

    @pl.when(e == 0)
    def _():
        acc_scr[...] = jnp.zeros(acc_scr.shape, jnp.float32)
        act_b[...] = jnp.zeros(act_b.shape, jnp.float32)

    args = (h_ref, u_ref, vt_ref, b_ref, beta_ref, a_ref, acc_scr, w_scr)

    @pl.when(e % 2 == 0)
    def _():
        _peer_main_step(*args, act_a, act_b)

    @pl.when(e % 2 == 1)
    def _():
        _peer_main_step(*args, act_b, act_a)

    @pl.when(e == pl.num_programs(1) - 1)
    def _():
        o_ref[...] = x1_ref[...] + acc_scr[...].T


def _peer_main(h2, u_bf, vt_bf, bs, betas, a_s, x1, tm=512, te=1024):
    m, d = h2.shape
    ne = u_bf.shape[0] // te
    ni = te // PEER_NKEYS
    prev = lambda e: jnp.maximum(e - 1, 0)
    return pl.pallas_call(
        _peer_main_kernel,
        grid=(m // tm, ne + 1),
        in_specs=[pl.BlockSpec((tm, d), lambda i, e: (i, 0)),
                  pl.BlockSpec((te, d), lambda i, e: (jnp.minimum(e, ne - 1), 0)),
                  pl.BlockSpec((d, te), lambda i, e: (0, prev(e))),
                  pl.BlockSpec((PEER_HEADS, PEER_NKEYS, tm), lambda i, e: (0, 0, i)),
                  pl.BlockSpec((PEER_HEADS, ni, tm), lambda i, e: (0, prev(e), i)),
                  pl.BlockSpec((PEER_HEADS, ni, tm), lambda i, e: (0, prev(e), i)),
                  pl.BlockSpec((tm, d), lambda i, e: (i, 0))],
        out_specs=pl.BlockSpec((tm, d), lambda i, e: (i, 0)),
        out_shape=jax.ShapeDtypeStruct((m, d), jnp.float32),
        scratch_shapes=[pltpu.VMEM((d, tm), jnp.float32),
                        pltpu.VMEM((te, tm), jnp.bfloat16),
                        pltpu.VMEM((te, tm), jnp.float32),
                        pltpu.VMEM((te, tm), jnp.float32)],
        compiler_params=_cparams(("parallel", "arbitrary")),
        name="peer_main",
    )(h2, u_bf, vt_bf, bs, betas, a_s, x1)


def _rope_tables(s):
    half = MLA_ROPE // 2
    freqs = ROPE_THETA ** (-2.0 * jnp.arange(half, dtype=jnp.float32) / MLA_ROPE)
    ang = jnp.arange(s, dtype=jnp.float32)[:, None] * freqs[None, :]
    cos, sin = jnp.cos(ang), jnp.sin(ang)
    z = jnp.zeros_like(cos)
    c = jnp.concatenate([cos, cos, z, z], axis=1)
    s1 = jnp.concatenate([z, sin, z, z], axis=1)
    s2 = jnp.concatenate([-sin, z, z, z], axis=1)
    return c, s1, s2


def _mla_mask_table():
    t = ATT_T
    r = np.arange(t)[:, None] // CHUNK
    c = np.arange(t)[None, :] // CHUNK
    return jnp.asarray(np.where(c <= r, 0.0, NEG).astype(np.float32))


def _layer(x, g_attn, w_in, g_cq, g_ckv, w_uq, w_ukv, g_mla_q, g_mla_k, g_dsa_q, g_dsa_k,
           rel_bias, w_o, g_ffn, w_pq, sub_keys, u_tab, v_tab):
    b, s, d = x.shape
    m = b * s
    bf = jnp.bfloat16
    f32 = jnp.float32
    assert s % ATT_T == 0 and m % 512 == 0

    o = np.cumsum([0, Q_LORA, KV_LORA, MLA_ROPE, 1024, 1024, 1024, 1024, IDX_HD, IDX_HEADS])
    zc = lambda n: jnp.zeros((d, n), f32)
    w_in_p = jnp.concatenate([
        w_in[:, o[3]:o[7]], w_in[:, o[0]:o[2]],
        w_in[:, o[2]:o[3]], zc(LANES - MLA_ROPE),
        w_in[:, o[7]:o[8]], zc(LANES - IDX_HD),
        w_in[:, o[8]:o[9]], zc(LANES - IDX_HEADS), zc(IN_PAD - COL_WI - LANES)], axis=1).astype(bf)
    wuq_p = jnp.pad(w_uq.reshape(Q_LORA, MLA_HEADS, MLA_QK),
                    ((0, 0), (0, 0), (0, MLA_QK_PAD - MLA_QK))).reshape(Q_LORA, -1).astype(bf)
    wukv_p = w_ukv.reshape(KV_LORA, MLA_HEADS, 2, MLA_NOPE).transpose(0, 2, 1, 3).reshape(KV_LORA, -1).astype(bf)
    gq_pad = jnp.pad(g_mla_q, (0, MLA_QK_PAD - MLA_QK))[None]
    gk_pad = jnp.pad(g_mla_k, (0, MLA_QK_PAD - MLA_QK))[None]
    rope_c, rope_s1, rope_s2 = _rope_tables(s)

    proj = _inproj(x.reshape(m, d), g_attn[None], w_in_p)
    proj3 = proj.reshape(b, s, IN_PAD)

    q_a, k_a, v_a = _mla_prep(proj3, g_cq[None], g_ckv[None], wuq_p, wukv_p, gq_pad, gk_pad,
                              rope_c, rope_s1, rope_s2)
    y_a = _mla_attn(q_a, k_a, v_a, _mla_mask_table())

    q_d, k_d, v_d, q_i, k_i2 = _dsa_prep(proj3, g_dsa_q[None], g_dsa_k[None])
    n_sel = min(DSA_TOPK, s // 4)
    sel = _dsa_index(q_i, proj3, k_i2, n_sel, COL_WI // LANES)
    y_b = _dsa_attn(q_d, k_d, v_d, sel, _bias_tiles(rel_bias))

    na = MLA_HEADS * MLA_V
    x1, h2 = _outproj(x.reshape(m, d), y_a.reshape(m, na), y_b.reshape(m, -1),
                      w_o[:na].astype(bf), w_o[na:].astype(bf), g_ffn[None])

    sk = sub_keys.reshape(PEER_HEADS * 2, PEER_NKEYS, -1).astype(bf)
    bs, betas, a_s = _peer_scores(h2, w_pq.T.astype(bf), sk)
    x2 = _peer_main(h2, u_tab.astype(bf), v_tab.T.astype(bf), bs, betas, a_s, x1)
    return x2.reshape(b, s, d)


def kernel(x, g_attn, w_in, g_cq, g_ckv, w_uq, w_ukv, g_mla_q, g_mla_k, g_dsa_q, g_dsa_k,
           rel_bias, w_o, g_ffn, w_pq, sub_keys, u_tab, v_tab):
    for l in range(w_in.shape[0]):
        x = _layer(x, g_attn[l], w_in[l], g_cq[l], g_ckv[l], w_uq[l], w_ukv[l], g_mla_q[l], g_mla_k[l],
                   g_dsa_q[l], g_dsa_k[l], rel_bias, w_o[l], g_ffn[l], w_pq[l], sub_keys[l],
                   u_tab[l], v_tab[l])
    return x
```

```python
import functools
import math

import numpy as np
import jax
import jax.numpy as jnp
from jax import lax
from jax.experimental import pallas as pl
from jax.experimental.pallas import tpu as pltpu

EPS = 1e-6
CHUNK = 64

MLA_HEADS = 8
MLA_NOPE = 128
MLA_ROPE = 64
MLA_V = 128
MLA_QK = MLA_NOPE + MLA_ROPE
MLA_QK_PAD = 256
Q_LORA = 512
KV_LORA = 256
ROPE_THETA = 10000.0

DSA_HEADS = 8
DSA_HD = 128
IDX_HEADS = 16
IDX_HD = 64
DSA_TOPK = 256

REL_BUCKETS = 32
REL_MAX_DIST = 128

PEER_HEADS = 8
PEER_NKEYS = 128
PEER_TOPK = 16

LANES = 128
NEG = -1e30
VMEM_LIMIT = 56 * 1024 * 1024

COL_Q, COL_K, COL_V, COL_QI = 0, 1024, 2048, 3072
COL_CQ, COL_CKV, COL_KR, COL_KI, COL_WI = 4096, 4608, 4864, 4992, 5120
IN_PAD = 5376

ATT_T = 256

_NT = (((1,), (1,)), ((), ()))


def _cparams(sem):
    return pltpu.CompilerParams(dimension_semantics=sem, vmem_limit_bytes=VMEM_LIMIT)


def _rep(x, n):
    return x if n == 1 else jnp.concatenate([x] * n, axis=1)


def _inproj_kernel(x_ref, g_ref, w_ref, o_ref, h_scr):
    @pl.when(pl.program_id(1) == 0)
    def _():
        x = x_ref[...]
        ms = jnp.mean(x * x, axis=-1, keepdims=True)
        h_scr[...] = (x * lax.rsqrt(ms + EPS) * g_ref[...]).astype(jnp.bfloat16)

    o_ref[...] = jnp.dot(h_scr[...], w_ref[...], preferred_element_type=jnp.float32)


def _inproj(x2d, g, w_p, tm=1024, tn=768):
    m, d = x2d.shape
    tm = min(tm, m)
    n = w_p.shape[1]
    return pl.pallas_call(
        _inproj_kernel,
        grid=(m // tm, n // tn),
        in_specs=[pl.BlockSpec((tm, d), lambda i, j: (i, 0)),
                  pl.BlockSpec((1, d), lambda i, j: (0, 0)),
                  pl.BlockSpec((d, tn), lambda i, j: (0, j))],
        out_specs=pl.BlockSpec((tm, tn), lambda i, j: (i, j)),
        out_shape=jax.ShapeDtypeStruct((m, n), jnp.float32),
        scratch_shapes=[pltpu.VMEM((tm, d), jnp.bfloat16)],
        compiler_params=_cparams(("parallel", "arbitrary")),
        name="inproj",
    )(x2d, g, w_p)


def _rope128(xr, c, s1, s2):
    return xr * c + pltpu.roll(xr, 96, 1) * s2 + pltpu.roll(xr, 32, 1) * s1


def _mla_prep_kernel(cq_ref, ckv_ref, kr_ref, gcq_ref, gckv_ref, wuq_ref, wukv_ref, gq_ref, gk_ref,
                     c_ref, s1_ref, s2_ref, q_ref, k_ref, v_ref):
    cq = cq_ref[0]
    cqn = (cq * lax.rsqrt(jnp.mean(cq * cq, axis=-1, keepdims=True) + EPS) * gcq_ref[...])
    q_all = jnp.dot(cqn.astype(jnp.bfloat16), wuq_ref[...], preferred_element_type=jnp.float32)
    ckv = ckv_ref[0]
    ckvn = (ckv * lax.rsqrt(jnp.mean(ckv * ckv, axis=-1, keepdims=True) + EPS) * gckv_ref[...])
    kv = jnp.dot(ckvn.astype(jnp.bfloat16), wukv_ref[...], preferred_element_type=jnp.float32)
    kr = kr_ref[0]
    kr_ss = jnp.sum(kr * kr, axis=-1, keepdims=True)
    c, s1, s2 = c_ref[...], s1_ref[...], s2_ref[...]
    gq, gk = gq_ref[...], gk_ref[...]
    scale = MLA_QK ** -0.5
    for h in range(MLA_HEADS):
        qh = q_all[:, h * MLA_QK_PAD:(h + 1) * MLA_QK_PAD]
        rq = lax.rsqrt(jnp.sum(qh * qh, axis=-1, keepdims=True) * (1.0 / MLA_QK) + EPS)
        qn = qh * rq * gq
        q_ref[0, h, :, :MLA_NOPE] = (qn[:, :MLA_NOPE] * scale).astype(jnp.bfloat16)
        q_ref[0, h, :, MLA_NOPE:] = (_rope128(qn[:, MLA_NOPE:], c, s1, s2) * scale).astype(jnp.bfloat16)
        kn = kv[:, h * MLA_NOPE:(h + 1) * MLA_NOPE]
        rk = lax.rsqrt((jnp.sum(kn * kn, axis=-1, keepdims=True) + kr_ss) * (1.0 / MLA_QK) + EPS)
        k_ref[0, h, :, :MLA_NOPE] = (kn * rk * gk[:, :MLA_NOPE]).astype(jnp.bfloat16)
        k_ref[0, h, :, MLA_NOPE:] = _rope128(kr * rk * gk[:, MLA_NOPE:], c, s1, s2).astype(jnp.bfloat16)
        v_ref[0, h] = kv[:, MLA_HEADS * MLA_NOPE + h * MLA_V:MLA_HEADS * MLA_NOPE + (h + 1) * MLA_V
                         ].astype(jnp.bfloat16)


def _mla_prep(proj3, g_cq, g_ckv, wuq_p, wukv_p, gq_pad, gk_pad, rope_c, rope_s1, rope_s2, tm=256):
    b, s, _ = proj3.shape
    const = lambda shape: pl.BlockSpec(shape, lambda bi, i: (0,) * len(shape))
    return pl.pallas_call(
        _mla_prep_kernel,
        grid=(b, s // tm),
        in_specs=[pl.BlockSpec((1, tm, Q_LORA), lambda bi, i: (bi, i, COL_CQ // Q_LORA)),
                  pl.BlockSpec((1, tm, KV_LORA), lambda bi, i: (bi, i, COL_CKV // KV_LORA)),
                  pl.BlockSpec((1, tm, LANES), lambda bi, i: (bi, i, COL_KR // LANES)),
                  const((1, Q_LORA)), const((1, KV_LORA)),
                  const(wuq_p.shape), const(wukv_p.shape),
                  const((1, MLA_QK_PAD)), const((1, MLA_QK_PAD)),
                  pl.BlockSpec((tm, LANES), lambda bi, i: (i, 0)),
                  pl.BlockSpec((tm, LANES), lambda bi, i: (i, 0)),
                  pl.BlockSpec((tm, LANES), lambda bi, i: (i, 0))],
        out_specs=[pl.BlockSpec((1, MLA_HEADS, tm, MLA_QK_PAD), lambda bi, i: (bi, 0, i, 0)),
                   pl.BlockSpec((1, MLA_HEADS, tm, MLA_QK_PAD), lambda bi, i: (bi, 0, i, 0)),
                   pl.BlockSpec((1, MLA_HEADS, tm, MLA_V), lambda bi, i: (bi, 0, i, 0))],
        out_shape=[jax.ShapeDtypeStruct((b, MLA_HEADS, s, MLA_QK_PAD), jnp.bfloat16),
                   jax.ShapeDtypeStruct((b, MLA_HEADS, s, MLA_QK_PAD), jnp.bfloat16),
                   jax.ShapeDtypeStruct((b, MLA_HEADS, s, MLA_V), jnp.bfloat16)],
        compiler_params=_cparams(("parallel", "parallel")),
        name="mla_prep",
    )(proj3, proj3, proj3, g_cq, g_ckv, wuq_p, wukv_p, gq_pad, gk_pad, rope_c, rope_s1, rope_s2)


def _dsa_prep_kernel(q_ref, k_ref, v_ref, qi_ref, ki_ref, gq_ref, gk_ref,
                     qo_ref, ko_ref, vo_ref, qio_ref, kio_ref):
    gq, gk = gq_ref[...], gk_ref[...]
    scale = DSA_HD ** -0.5
    for h in range(DSA_HEADS):
        sl = slice(h * DSA_HD, (h + 1) * DSA_HD)
        qh = q_ref[0, :, sl]
        qo_ref[0, h] = (qh * lax.rsqrt(jnp.mean(qh * qh, axis=-1, keepdims=True) + EPS) * gq * scale
                        ).astype(jnp.bfloat16)
        kh = k_ref[0, :, sl]
        ko_ref[0, h] = (kh * lax.rsqrt(jnp.mean(kh * kh, axis=-1, keepdims=True) + EPS) * gk
                        ).astype(jnp.bfloat16)
        vo_ref[0, h] = v_ref[0, :, sl].astype(jnp.bfloat16)
    qio_ref[0] = (qi_ref[0] * (IDX_HD ** -0.5)).astype(jnp.bfloat16)
    ki = ki_ref[0]
    kio_ref[0, 0] = ki.astype(jnp.bfloat16)
    kio_ref[0, 1] = pltpu.roll(ki, IDX_HD, 1).astype(jnp.bfloat16)


def _dsa_prep(proj3, gq, gk, tm=256):
    b, s, _ = proj3.shape
    w = DSA_HEADS * DSA_HD
    hd_spec = pl.BlockSpec((1, DSA_HEADS, tm, DSA_HD), lambda bi, i: (bi, 0, i, 0))
    hd_shape = jax.ShapeDtypeStruct((b, DSA_HEADS, s, DSA_HD), jnp.bfloat16)
    return pl.pallas_call(
        _dsa_prep_kernel,
        grid=(b, s // tm),
        in_specs=[pl.BlockSpec((1, tm, w), lambda bi, i: (bi, i, COL_Q // w)),
                  pl.BlockSpec((1, tm, w), lambda bi, i: (bi, i, COL_K // w)),
                  pl.BlockSpec((1, tm, w), lambda bi, i: (bi, i, COL_V // w)),
                  pl.BlockSpec((1, tm, w), lambda bi, i: (bi, i, COL_QI // w)),
                  pl.BlockSpec((1, tm, LANES), lambda bi, i: (bi, i, COL_KI // LANES)),
                  pl.BlockSpec((1, DSA_HD), lambda bi, i: (0, 0)),
                  pl.BlockSpec((1, DSA_HD), lambda bi, i: (0, 0))],
        out_specs=[hd_spec, hd_spec, hd_spec,
                   pl.BlockSpec((1, tm, w), lambda bi, i: (bi, i, 0)),
                   pl.BlockSpec((1, 2, tm, LANES), lambda bi, i: (bi, 0, i, 0))],
        out_shape=[hd_shape, hd_shape, hd_shape,
                   jax.ShapeDtypeStruct((b, s, w), jnp.bfloat16),
                   jax.ShapeDtypeStruct((b, 2, s, LANES), jnp.bfloat16)],
        compiler_params=_cparams(("parallel", "parallel")),
        name="dsa_prep",
    )(proj3, proj3, proj3, proj3, proj3, gq, gk)


def _lane_fold(x, op):
    out = x[:, :LANES]
    for c in range(1, x.shape[1] // LANES):
        out = op(out, x[:, c * LANES:(c + 1) * LANES])
    return out


ATT_SLOTS = 8


def _attn_scratch(nsub):
    t = ATT_T
    return ([pltpu.VMEM((nsub, t, LANES), jnp.float32)] * 3
            + [pltpu.VMEM((ATT_SLOTS, t, t), jnp.float32), pltpu.VMEM((ATT_SLOTS, t, t), jnp.bfloat16)])


def _attn_sweep(q_ref, k_ref, v_ref, o_ref, m_scr, l_scr, acc_scr, s_scr, p_scr, add_fn):
    qi = pl.program_id(2)
    t = ATT_T
    nsub = q_ref.shape[2] // t
    nrep = t // LANES
    nslot = s_scr.shape[0]
    first = nsub * qi

    def qk(slot, unit):
        r, ki, _ = unit
        off = pl.multiple_of(ki * t, t)
        s_scr[slot] = lax.dot_general(q_ref[0, 0, r * t:(r + 1) * t, :], k_ref[0, 0, pl.ds(off, t), :],
                                      _NT, preferred_element_type=jnp.float32)

    def logits(slot, unit):
        a = add_fn(*unit)
        return s_scr[slot] if a is None else s_scr[slot] + a

    def fold_max(slot, unit):
        r = unit[0]
        m_scr[r] = jnp.maximum(m_scr[r], _lane_fold(logits(slot, unit), jnp.maximum))

    def probs(slot, unit):
        r = unit[0]
        p = jnp.exp(logits(slot, unit) - _rep(m_scr[r], nrep))
        l_scr[r] += _lane_fold(p, jnp.add)
        p_scr[slot] = p.astype(jnp.bfloat16)

    def pv(slot, unit):
        r, ki, _ = unit
        off = pl.multiple_of(ki * t, t)
        acc_scr[r] += jnp.dot(p_scr[slot], v_ref[0, 0, pl.ds(off, t), :], preferred_element_type=jnp.float32)

    def run(units, stages):
        n = len(units)
        for step in range(n + len(stages) - 1):
            for k, stage in enumerate(stages):
                i = step - k
                if 0 <= i < n:
                    stage(i % nslot, units[i])

    def sweep(stages):
        def body(kp, carry):
            run([(r, 2 * kp + kk, None) for kk in range(2) for r in range(nsub)], stages)
            return carry

        if nsub % 2 == 0:
            lax.fori_loop(0, first // 2, body, 0)
        else:
            lax.fori_loop(0, first, lambda ki, c: (run([(0, ki, None)], stages), c)[1], 0)
        run([(r, first + j, j - r) for j in range(nsub) for r in range(j, nsub)], stages)

    m_scr[...] = jnp.full(m_scr.shape, NEG, jnp.float32)
    sweep((qk, fold_max))
    for r in range(nsub):
        m_scr[r] = jnp.broadcast_to(jnp.max(m_scr[r], axis=1, keepdims=True), (t, LANES))
    l_scr[...] = jnp.zeros(l_scr.shape, jnp.float32)
    acc_scr[...] = jnp.zeros(acc_scr.shape, jnp.float32)
    sweep((qk, probs, pv))
    for r in range(nsub):
        l = jnp.sum(l_scr[r], axis=1, keepdims=True)
        o_ref[0, r * t:(r + 1) * t, :] = (acc_scr[r] / l).astype(o_ref.dtype)


def _attn_nsub(n_tiles):
    return 4 if n_tiles % 4 == 0 else (2 if n_tiles % 2 == 0 else 1)


def _mla_attn_kernel(q_ref, k_ref, v_ref, mask_ref, o_ref, *scr):
    def add_fn(r, ki, rel):
        return mask_ref[...] if rel == 0 else None

    _attn_sweep(q_ref, k_ref, v_ref, o_ref, *scr, add_fn)


def _mla_attn(q, k, v, mask_tbl):
    b, h, s, dq = q.shape
    t = ATT_T
    nsub = _attn_nsub(s // t)
    tq = nsub * t
    return pl.pallas_call(
        _mla_attn_kernel,
        grid=(b, h, s // tq),
        in_specs=[pl.BlockSpec((1, 1, tq, dq), lambda bi, hi, i: (bi, hi, i, 0)),
                  pl.BlockSpec((1, 1, s, dq), lambda bi, hi, i: (bi, hi, 0, 0)),
                  pl.BlockSpec((1, 1, s, MLA_V), lambda bi, hi, i: (bi, hi, 0, 0)),
                  pl.BlockSpec((t, t), lambda bi, hi, i: (0, 0))],
        out_specs=pl.BlockSpec((1, tq, MLA_V), lambda bi, hi, i: (bi, i, hi)),
        out_shape=jax.ShapeDtypeStruct((b, s, h * MLA_V), jnp.bfloat16),
        scratch_shapes=_attn_scratch(nsub),
        compiler_params=_cparams(("parallel", "parallel", "arbitrary")),
        name="mla_attn",
    )(q, k, v, mask_tbl)


def _dsa_attn_kernel(q_ref, k_ref, v_ref, sel_ref, bias_ref, o_ref, *scr):
    nsub = q_ref.shape[2] // ATT_T
    first = nsub * pl.program_id(2)

    def add_fn(r, ki, rel):
        back = jnp.minimum(first + r - ki, 2) if rel is None else min(-rel, 2)
        return bias_ref[0, back] + sel_ref[0, r, ki].astype(jnp.float32)

    _attn_sweep(q_ref, k_ref, v_ref, o_ref, *scr, add_fn)


def _dsa_attn(q, k, v, sel, bias_tbl):
    b, h, s, d = q.shape
    t = ATT_T
    n = s // t
    nsub = _attn_nsub(n)
    tq = nsub * t
    return pl.pallas_call(
        _dsa_attn_kernel,
        grid=(b, h, s // tq),
        in_specs=[pl.BlockSpec((1, 1, tq, d), lambda bi, hi, i: (bi, hi, i, 0)),
                  pl.BlockSpec((1, 1, s, d), lambda bi, hi, i: (bi, hi, 0, 0)),
                  pl.BlockSpec((1, 1, s, d), lambda bi, hi, i: (bi, hi, 0, 0)),
                  pl.BlockSpec((1, nsub, n, t, t), lambda bi, hi, i: (bi, i, 0, 0, 0)),
                  pl.BlockSpec((1, 3, t, t), lambda bi, hi, i: (hi, 0, 0, 0))],
        out_specs=pl.BlockSpec((1, tq, d), lambda bi, hi, i: (bi, i, hi)),
        out_shape=jax.ShapeDtypeStruct((b, s, h * d), jnp.bfloat16),
        scratch_shapes=_attn_scratch(nsub),
        compiler_params=_cparams(("parallel", "parallel", "arbitrary")),
        name="dsa_attn",
    )(q, k, v, sel, bias_tbl)


def _bias_kernel(rb_ref, o_ref):
    h = pl.program_id(0)
    t = o_ref.shape[2]
    row = lax.broadcasted_iota(jnp.int32, (t, t), 0)
    col = lax.broadcasted_iota(jnp.int32, (t, t), 1)
    nb = REL_BUCKETS // 2
    max_exact = nb // 2
    for r in range(2):
        rel = col - row - r * t
        n = jnp.abs(rel)
        n2 = n * n
        large = jnp.full((t, t), max_exact, jnp.int32)
        for j in range(1, nb - max_exact):
            large = large + (n2 >= (max_exact * max_exact) * (2 ** j)).astype(jnp.int32)
        bucket = jnp.where(rel > 0, nb, 0) + jnp.where(n < max_exact, n, large)
        val = jnp.zeros((t, t), jnp.float32)
        for bkt in range(REL_BUCKETS):
            val = jnp.where(bucket == bkt, rb_ref[bkt, h], val)
        o_ref[0, r] = val
    o_ref[0, 2] = jnp.full((t, t), rb_ref[nb - 1, h], jnp.float32)


def _bias_tiles(rel_bias):
    t = ATT_T
    assert t >= REL_MAX_DIST
    return pl.pallas_call(
        _bias_kernel,
        grid=(DSA_HEADS,),
        in_specs=[pl.BlockSpec(memory_space=pltpu.SMEM)],
        out_specs=pl.BlockSpec((1, 3, t, t), lambda h: (h, 0, 0, 0)),
        out_shape=jax.ShapeDtypeStruct((DSA_HEADS, 3, t, t), jnp.float32),
        compiler_params=_cparams(("arbitrary",)),
        name="t5_bias",
    )(rel_bias)


_F32_INF_BITS = 0x7F800000
_INT_MIN = np.int32(-2 ** 31)


def _index_kernel(n_sel, q_ref, w_ref, k_ref, o_ref, score_scr, wt_scr):
    qi = pl.program_id(1)
    t = q_ref.shape[1]
    n_tiles = score_scr.shape[0]
    n_adm = qi + 1
    wt_scr[...] = w_ref[0].T
    q_chunk = (qi * t + lax.broadcasted_iota(jnp.int32, (1, t), 1)) // CHUNK

    def score_body(c, carry):
        off = pl.multiple_of(c * t, t)
        acc = jnp.zeros((t, t), jnp.float32)
        for hp in range(IDX_HEADS // 2):
            qpair = q_ref[0, :, hp * LANES:(hp + 1) * LANES]
            for par in range(2):
                h = 2 * hp + par
                kk = k_ref[0, par, pl.ds(off, t), :]
                d = lax.dot_general(kk, qpair, _NT, preferred_element_type=jnp.float32)
                acc = acc + jnp.maximum(d, 0.0) * wt_scr[h:h + 1, :]
        k_chunk = (c * t + lax.broadcasted_iota(jnp.int32, (t, 1), 0)) // CHUNK
        score_scr[c] = jnp.where(k_chunk <= q_chunk, acc, -jnp.inf)
        return carry

    lax.fori_loop(0, n_adm, score_body, 0)

    def count_ge(cand):
        def cnt_body(c, cnt):
            ge = jnp.where(score_scr[c] >= cand, 1.0, 0.0)
            return cnt + jnp.sum(ge.reshape(4, t // 32, 8, t), axis=1)

        cnt = lax.fori_loop(0, n_adm, cnt_body, jnp.zeros((4, 8, t), jnp.float32))
        return jnp.sum(jnp.sum(cnt, axis=0), axis=0, keepdims=True)

    kf = float(n_sel)
    pos = count_ge(jnp.zeros((1, t), jnp.float32)) >= kf

    def thr_of(u):
        neg_bits = jnp.bitwise_or(_F32_INF_BITS - u, _INT_MIN)
        return pltpu.bitcast(jnp.where(pos, u, neg_bits), jnp.float32)

    def bit_body(it, u):
        cand = u + jnp.left_shift(jnp.int32(1), 30 - it)
        ok = jnp.logical_and(cand <= _F32_INF_BITS, count_ge(thr_of(cand)) >= kf)
        return jnp.where(ok, cand, u)

    thr = thr_of(lax.fori_loop(0, 31, bit_body, jnp.zeros((1, t), jnp.int32)))

    def out_body(c, carry):
        score = score_scr[c]
        sel = jnp.logical_and(score >= thr, score > -jnp.inf)
        o_ref[0, 0, c] = jnp.where(sel, 0.0, NEG).T.astype(o_ref.dtype)
        return carry

    lax.fori_loop(0, n_adm, out_body, 0)

    def fill_body(c, carry):
        o_ref[0, 0, c] = jnp.full((t, t), NEG, o_ref.dtype)
        return carry

    lax.fori_loop(n_adm, n_tiles, fill_body, 0)


def _dsa_index(q_idx, w3, k_idx2, n_sel, w_col_block):
    b, s, _ = q_idx.shape
    t = ATT_T
    n = s // t
    return pl.pallas_call(
        functools.partial(_index_kernel, n_sel),
        grid=(b, n),
        in_specs=[pl.BlockSpec((1, t, IDX_HEADS * IDX_HD), lambda bi, i: (bi, i, 0)),
                  pl.BlockSpec((1, t, LANES), lambda bi, i: (bi, i, w_col_block)),
                  pl.BlockSpec((1, 2, s, LANES), lambda bi, i: (bi, 0, 0, 0))],
        out_specs=pl.BlockSpec((1, 1, n, t, t), lambda bi, i: (bi, i, 0, 0, 0)),
        out_shape=jax.ShapeDtypeStruct((b, n, n, t, t), jnp.bfloat16),
        scratch_shapes=[pltpu.VMEM((n, t, t), jnp.float32),
                        pltpu.VMEM((LANES, t), jnp.float32)],
        compiler_params=_cparams(("parallel", "parallel")),
        name="dsa_index",
    )(q_idx, w3, k_idx2)


def _outproj_kernel(x_ref, ya_ref, yb_ref, wa_ref, wb_ref, g_ref, x1_ref, h2_ref):
    y = (jnp.dot(ya_ref[...], wa_ref[...], preferred_element_type=jnp.float32)
         + jnp.dot(yb_ref[...], wb_ref[...], preferred_element_type=jnp.float32))
    x1 = x_ref[...] + y
    x1_ref[...] = x1
    ms = jnp.mean(x1 * x1, axis=-1, keepdims=True)
    h2_ref[...] = (x1 * lax.rsqrt(ms + EPS) * g_ref[...]).astype(jnp.bfloat16)


def _outproj(x2d, ya, yb, wo_a, wo_b, g_ffn, tm=256):
    m, d = x2d.shape
    ka, kb = ya.shape[1], yb.shape[1]
    return pl.pallas_call(
        _outproj_kernel,
        grid=(m // tm,),
        in_specs=[pl.BlockSpec((tm, d), lambda i: (i, 0)),
                  pl.BlockSpec((tm, ka), lambda i: (i, 0)),
                  pl.BlockSpec((tm, kb), lambda i: (i, 0)),
                  pl.BlockSpec((ka, d), lambda i: (0, 0)),
                  pl.BlockSpec((kb, d), lambda i: (0, 0)),
                  pl.BlockSpec((1, d), lambda i: (0, 0))],
        out_specs=[pl.BlockSpec((tm, d), lambda i: (i, 0)),
                   pl.BlockSpec((tm, d), lambda i: (i, 0))],
        out_shape=[jax.ShapeDtypeStruct((m, d), jnp.float32),
                   jax.ShapeDtypeStruct((m, d), jnp.bfloat16)],
        compiler_params=_cparams(("parallel",)),
        name="outproj",
    )(x2d, ya, yb, wo_a, wo_b, g_ffn)


def _top_values(cur, k):
    out = []
    for _ in range(k):
        m = jnp.max(cur, axis=0, keepdims=True)
        out.append(m)
        cur = jnp.where(cur == m, -jnp.inf, cur)
    return out


_N_TOP = PEER_TOPK + 1
_CAND_PAIRS = [(a, b) for a in range(_N_TOP) for b in range(_N_TOP) if (a + 1) * (b + 1) <= _N_TOP]


def _peer_score_kernel(h_ref, wq_ref, sk_ref, b_ref, beta_ref, a_ref):
    tm = h_ref.shape[0]
    qr_t = lax.dot_general(wq_ref[...], h_ref[...], _NT, preferred_element_type=jnp.float32)
    for h in range(PEER_HEADS):
        q0 = qr_t[(2 * h) * PEER_NKEYS:(2 * h + 1) * PEER_NKEYS].astype(jnp.bfloat16)
        q1 = qr_t[(2 * h + 1) * PEER_NKEYS:(2 * h + 2) * PEER_NKEYS].astype(jnp.bfloat16)
        s0 = jnp.dot(sk_ref[2 * h], q0, preferred_element_type=jnp.float32)
        s1 = jnp.dot(sk_ref[2 * h + 1], q1, preferred_element_type=jnp.float32)
        top0 = _top_values(s0, _N_TOP)
        top1 = _top_values(s1, _N_TOP)
        cands = [top0[a] + top1[b] for (a, b) in _CAND_PAIRS]
        pad = (-len(cands)) % 8
        cands += [jnp.full((1, tm), -jnp.inf, jnp.float32)] * pad
        vals = _top_values(jnp.concatenate(cands, axis=0), _N_TOP)
        z = jnp.zeros((1, tm), jnp.float32)
        for v in vals[:PEER_TOPK]:
            z = z + jnp.exp(v - vals[0])
        cut = 0.5 * (vals[PEER_TOPK - 1] + vals[PEER_TOPK])
        b_ref[h] = jnp.exp(s1 - top1[0])
        beta_ref[h] = jnp.exp((cut - top1[0]) - s0)
        a_ref[h] = jnp.exp(s0 - top0[0]) / z


def _peer_scores(h2, wpq_t, sk, tm=256):
    m, d = h2.shape
    nq = wpq_t.shape[0]
    out_spec = pl.BlockSpec((PEER_HEADS, PEER_NKEYS, tm), lambda i: (0, 0, i))
    out_shape = jax.ShapeDtypeStruct((PEER_HEADS, PEER_NKEYS, m), jnp.float32)
    return pl.pallas_call(
        _peer_score_kernel,
        grid=(m // tm,),
        in_specs=[pl.BlockSpec((tm, d), lambda i: (i, 0)),
                  pl.BlockSpec((nq, d), lambda i: (0, 0)),
                  pl.BlockSpec(sk.shape, lambda i: (0, 0, 0))],
        out_specs=[out_spec, out_spec, out_spec],
        out_shape=[out_shape, out_shape, out_shape],
        compiler_params=_cparams(("parallel",)),
        name="peer_scores",
    )(h2, wpq_t, sk)


def _gelu_exact(x):
    return 0.5 * x * (1.0 + lax.erf(x * (1.0 / math.sqrt(2.0))))


PEER_KC = 256


def _peer_main_step(h_ref, u_ref, vt_ref, b_ref, beta_ref, a_ref, acc_scr, w_scr, act_next, act_cur):
    tm = h_ref.shape[0]
    te = u_ref.shape[0]
    act_next[...] = lax.dot_general(u_ref[...], h_ref[...], _NT, preferred_element_type=jnp.float32)
    parts = []
    for c in range(te // PEER_KC):
        for il in range(c * PEER_KC // PEER_NKEYS, (c + 1) * PEER_KC // PEER_NKEYS):
            rows = slice(il * PEER_NKEYS, (il + 1) * PEER_NKEYS)
            for lt in range(tm // LANES):
                lanes = slice(lt * LANES, (lt + 1) * LANES)
                g = None
                for h in range(PEER_HEADS):
                    b = b_ref[h, :, lanes]
                    term = a_ref[h, il:il + 1, lanes] * jnp.where(b >= beta_ref[h, il:il + 1, lanes], b, 0.0)
                    g = term if g is None else g + term
                w_scr[rows, lanes] = (g * _gelu_exact(act_cur[rows, lanes])).astype(jnp.bfloat16)
        ck = slice(c * PEER_KC, (c + 1) * PEER_KC)
        parts.append(jnp.dot(vt_ref[:, ck], w_scr[ck, :], preferred_element_type=jnp.float32))
    acc_scr[...] += functools.reduce(lambda p, q: p + q, parts)


def _peer_main_kernel(h_ref, u_ref, vt_ref, b_ref, beta_ref, a_ref, x1_ref, o_ref,
                      acc_scr, w_scr, act_a, act_b):
    e = pl.program_id(1)

    @pl.when(e == 0)
    def _():
        acc_scr[...] = jnp.zeros(acc_scr.shape, jnp.float32)
        act_b[...] = jnp.zeros(act_b.shape, jnp.float32)

    args = (h_ref, u_ref, vt_ref, b_ref, beta_ref, a_ref, acc_scr, w_scr)

    @pl.when(e % 2 == 0)
    def _():
        _peer_main_step(*args, act_a, act_b)

    @pl.when(e % 2 == 1)
    def _():
        _peer_main_step(*args, act_b, act_a)

    @pl.when(e == pl.num_programs(1) - 1)
    def _():
        o_ref[...] = x1_ref[...] + acc_scr[...].T


def _peer_main(h2, u_bf, vt_bf, bs, betas, a_s, x1, tm=512, te=1024):
    m, d = h2.shape
    ne = u_bf.shape[0] // te
    ni = te // PEER_NKEYS
    back = lambda e, n: jnp.clip(e - n, 0, ne - 1)
    once = pl.Buffered(1)
    return pl.pallas_call(
        _peer_main_kernel,
        grid=(m // tm, ne + 1),
        in_specs=[pl.BlockSpec((tm, d), lambda i, e: (i, 0), pipeline_mode=once),
                  pl.BlockSpec((te, d), lambda i, e: (back(e, 0), 0)),
                  pl.BlockSpec((d, te), lambda i, e: (0, back(e, 1))),
                  pl.BlockSpec((PEER_HEADS, PEER_NKEYS, tm), lambda i, e: (0, 0, i), pipeline_mode=once),
                  pl.BlockSpec((PEER_HEADS, ni, tm), lambda i, e: (0, back(e, 1), i)),
                  pl.BlockSpec((PEER_HEADS, ni, tm), lambda i, e: (0, back(e, 1), i)),
                  pl.BlockSpec((tm, d), lambda i, e: (i, 0), pipeline_mode=once)],
        out_specs=pl.BlockSpec((tm, d), lambda i, e: (i, 0)),
        out_shape=jax.ShapeDtypeStruct((m, d), jnp.float32),
        scratch_shapes=[pltpu.VMEM((d, tm), jnp.float32),
                        pltpu.VMEM((te, tm), jnp.bfloat16),
                        pltpu.VMEM((te, tm), jnp.float32),
                        pltpu.VMEM((te, tm), jnp.float32)],
        compiler_params=_cparams(("parallel", "arbitrary")),
        name="peer_main",
    )(h2, u_bf, vt_bf, bs, betas, a_s, x1)


def _rope_tables(s):
    half = MLA_ROPE // 2
    freqs = ROPE_THETA ** (-2.0 * jnp.arange(half, dtype=jnp.float32) / MLA_ROPE)
    ang = jnp.arange(s, dtype=jnp.float32)[:, None] * freqs[None, :]
    cos, sin = jnp.cos(ang), jnp.sin(ang)
    z = jnp.zeros_like(cos)
    c = jnp.concatenate([cos, cos, z, z], axis=1)
    s1 = jnp.concatenate([z, sin, z, z], axis=1)
    s2 = jnp.concatenate([-sin, z, z, z], axis=1)
    return c, s1, s2


def _mla_mask_table():
    t = ATT_T
    r = np.arange(t)[:, None] // CHUNK
    c = np.arange(t)[None, :] // CHUNK
    return jnp.asarray(np.where(c <= r, 0.0, NEG).astype(np.float32))


def _layer(x, g_attn, w_in, g_cq, g_ckv, w_uq, w_ukv, g_mla_q, g_mla_k, g_dsa_q, g_dsa_k,
           rel_bias, w_o, g_ffn, w_pq, sub_keys, u_tab, v_tab):
    b, s, d = x.shape
    m = b * s
    bf = jnp.bfloat16
    f32 = jnp.float32
    assert s % ATT_T == 0 and m % 512 == 0

    o = np.cumsum([0, Q_LORA, KV_LORA, MLA_ROPE, 1024, 1024, 1024, 1024, IDX_HD, IDX_HEADS])
    zc = lambda n: jnp.zeros((d, n), f32)
    w_in_p = jnp.concatenate([
        w_in[:, o[3]:o[7]], w_in[:, o[0]:o[2]],
        w_in[:, o[2]:o[3]], zc(LANES - MLA_ROPE),
        w_in[:, o[7]:o[8]], zc(LANES - IDX_HD),
        w_in[:, o[8]:o[9]], zc(LANES - IDX_HEADS), zc(IN_PAD - COL_WI - LANES)], axis=1).astype(bf)
    wuq_p = jnp.pad(w_uq.reshape(Q_LORA, MLA_HEADS, MLA_QK),
                    ((0, 0), (0, 0), (0, MLA_QK_PAD - MLA_QK))).reshape(Q_LORA, -1).astype(bf)
    wukv_p = w_ukv.reshape(KV_LORA, MLA_HEADS, 2, MLA_NOPE).transpose(0, 2, 1, 3).reshape(KV_LORA, -1).astype(bf)
    gq_pad = jnp.pad(g_mla_q, (0, MLA_QK_PAD - MLA_QK))[None]
    gk_pad = jnp.pad(g_mla_k, (0, MLA_QK_PAD - MLA_QK))[None]
    rope_c, rope_s1, rope_s2 = _rope_tables(s)

    proj = _inproj(x.reshape(m, d), g_attn[None], w_in_p)
    proj3 = proj.reshape(b, s, IN_PAD)

    q_a, k_a, v_a = _mla_prep(proj3, g_cq[None], g_ckv[None], wuq_p, wukv_p, gq_pad, gk_pad,
                              rope_c, rope_s1, rope_s2)
    y_a = _mla_attn(q_a, k_a, v_a, _mla_mask_table())

    q_d, k_d, v_d, q_i, k_i2 = _dsa_prep(proj3, g_dsa_q[None], g_dsa_k[None])
    n_sel = min(DSA_TOPK, s // 4)
    sel = _dsa_index(q_i, proj3, k_i2, n_sel, COL_WI // LANES)
    y_b = _dsa_attn(q_d, k_d, v_d, sel, _bias_tiles(rel_bias))

    na = MLA_HEADS * MLA_V
    x1, h2 = _outproj(x.reshape(m, d), y_a.reshape(m, na), y_b.reshape(m, -1),
                      w_o[:na].astype(bf), w_o[na:].astype(bf), g_ffn[None])

    sk = sub_keys.reshape(PEER_HEADS * 2, PEER_NKEYS, -1).astype(bf)
    bs, betas, a_s = _peer_scores(h2, w_pq.T.astype(bf), sk)
    x2 = _peer_main(h2, u_tab.astype(bf), v_tab.T.astype(bf), bs, betas, a_s, x1)
    return x2.reshape(b, s, d)


def kernel(x, g_attn, w_in, g_cq, g_ckv, w_uq, w_ukv, g_mla_q, g_mla_k, g_dsa_q, g_dsa_k,
           rel_bias, w_o, g_ffn, w_pq, sub_keys, u_tab, v_tab):
    for l in range(w_in.shape[0]):
        x = _layer(x, g_attn[l], w_in[l], g_cq[l], g_ckv[l], w_uq[l], w_ukv[l], g_mla_q[l], g_mla_k[l],
                   g_dsa_q[l], g_dsa_k[l], rel_bias, w_o[l], g_ffn[l], w_pq[l], sub_keys[l],
                   u_tab[l], v_tab[l])
    return x
```

```python
import functools
import math

import numpy as np
import jax
import jax.numpy as jnp
from jax import lax
from jax.experimental import pallas as pl
from jax.experimental.pallas import tpu as pltpu

EPS = 1e-6
CHUNK = 64

MLA_HEADS = 8
MLA_NOPE = 128
MLA_ROPE = 64
MLA_V = 128
MLA_QK = MLA_NOPE + MLA_ROPE
MLA_QK_PAD = 256
Q_LORA = 512
KV_LORA = 256
ROPE_THETA = 10000.0

DSA_HEADS = 8
DSA_HD = 128
IDX_HEADS = 16
IDX_HD = 64
DSA_TOPK = 256

REL_BUCKETS = 32
REL_MAX_DIST = 128

PEER_HEADS = 8
PEER_NKEYS = 128
PEER_TOPK = 16

LANES = 128
NEG = -1e30
VMEM_LIMIT = 56 * 1024 * 1024

COL_Q, COL_K, COL_V, COL_QI = 0, 1024, 2048, 3072
COL_CQ, COL_CKV, COL_KR, COL_KI, COL_WI = 4096, 4608, 4864, 4992, 5120
IN_PAD = 5376

ATT_T = 256

_NT = (((1,), (1,)), ((), ()))


def _cparams(sem):
    return pltpu.CompilerParams(dimension_semantics=sem, vmem_limit_bytes=VMEM_LIMIT)


def _rep(x, n):
    return x if n == 1 else jnp.concatenate([x] * n, axis=1)


def _inproj_kernel(x_ref, g_ref, w_ref, o_ref, h_scr):
    @pl.when(pl.program_id(1) == 0)
    def _():
        x = x_ref[...]
        ms = jnp.mean(x * x, axis=-1, keepdims=True)
        h_scr[...] = (x * lax.rsqrt(ms + EPS) * g_ref[...]).astype(jnp.bfloat16)

    o_ref[...] = jnp.dot(h_scr[...], w_ref[...], preferred_element_type=jnp.float32)


def _inproj(x2d, g, w_p, tm=1024, tn=768):
    m, d = x2d.shape
    tm = min(tm, m)
    n = w_p.shape[1]
    return pl.pallas_call(
        _inproj_kernel,
        grid=(m // tm, n // tn),
        in_specs=[pl.BlockSpec((tm, d), lambda i, j: (i, 0)),
                  pl.BlockSpec((1, d), lambda i, j: (0, 0)),
                  pl.BlockSpec((d, tn), lambda i, j: (0, j))],
        out_specs=pl.BlockSpec((tm, tn), lambda i, j: (i, j)),
        out_shape=jax.ShapeDtypeStruct((m, n), jnp.float32),
        scratch_shapes=[pltpu.VMEM((tm, d), jnp.bfloat16)],
        compiler_params=_cparams(("parallel", "arbitrary")),
        name="inproj",
    )(x2d, g, w_p)


def _rope128(xr, c, s1, s2):
    return xr * c + pltpu.roll(xr, 96, 1) * s2 + pltpu.roll(xr, 32, 1) * s1


def _mla_prep_kernel(cq_ref, ckv_ref, kr_ref, gcq_ref, gckv_ref, wuq_ref, wukv_ref, gq_ref, gk_ref,
                     c_ref, s1_ref, s2_ref, q_ref, k_ref, v_ref):
    cq = cq_ref[0]
    cqn = (cq * lax.rsqrt(jnp.mean(cq * cq, axis=-1, keepdims=True) + EPS) * gcq_ref[...])
    q_all = jnp.dot(cqn.astype(jnp.bfloat16), wuq_ref[...], preferred_element_type=jnp.float32)
    ckv = ckv_ref[0]
    ckvn = (ckv * lax.rsqrt(jnp.mean(ckv * ckv, axis=-1, keepdims=True) + EPS) * gckv_ref[...])
    kv = jnp.dot(ckvn.astype(jnp.bfloat16), wukv_ref[...], preferred_element_type=jnp.float32)
    kr = kr_ref[0]
    kr_ss = jnp.sum(kr * kr, axis=-1, keepdims=True)
    c, s1, s2 = c_ref[...], s1_ref[...], s2_ref[...]
    gq, gk = gq_ref[...], gk_ref[...]
    scale = MLA_QK ** -0.5
    for h in range(MLA_HEADS):
        qh = q_all[:, h * MLA_QK_PAD:(h + 1) * MLA_QK_PAD]
        rq = lax.rsqrt(jnp.sum(qh * qh, axis=-1, keepdims=True) * (1.0 / MLA_QK) + EPS)
        qn = qh * rq * gq
        q_ref[0, h, :, :MLA_NOPE] = (qn[:, :MLA_NOPE] * scale).astype(jnp.bfloat16)
        q_ref[0, h, :, MLA_NOPE:] = (_rope128(qn[:, MLA_NOPE:], c, s1, s2) * scale).astype(jnp.bfloat16)
        kn = kv[:, h * MLA_NOPE:(h + 1) * MLA_NOPE]
        rk = lax.rsqrt((jnp.sum(kn * kn, axis=-1, keepdims=True) + kr_ss) * (1.0 / MLA_QK) + EPS)
        k_ref[0, h, :, :MLA_NOPE] = (kn * rk * gk[:, :MLA_NOPE]).astype(jnp.bfloat16)
        k_ref[0, h, :, MLA_NOPE:] = _rope128(kr * rk * gk[:, MLA_NOPE:], c, s1, s2).astype(jnp.bfloat16)
        v_ref[0, h] = kv[:, MLA_HEADS * MLA_NOPE + h * MLA_V:MLA_HEADS * MLA_NOPE + (h + 1) * MLA_V
                         ].astype(jnp.bfloat16)


def _mla_prep(proj3, g_cq, g_ckv, wuq_p, wukv_p, gq_pad, gk_pad, rope_c, rope_s1, rope_s2, tm=256):
    b, s, _ = proj3.shape
    const = lambda shape: pl.BlockSpec(shape, lambda bi, i: (0,) * len(shape))
    return pl.pallas_call(
        _mla_prep_kernel,
        grid=(b, s // tm),
        in_specs=[pl.BlockSpec((1, tm, Q_LORA), lambda bi, i: (bi, i, COL_CQ // Q_LORA)),
                  pl.BlockSpec((1, tm, KV_LORA), lambda bi, i: (bi, i, COL_CKV // KV_LORA)),
                  pl.BlockSpec((1, tm, LANES), lambda bi, i: (bi, i, COL_KR // LANES)),
                  const((1, Q_LORA)), const((1, KV_LORA)),
                  const(wuq_p.shape), const(wukv_p.shape),
                  const((1, MLA_QK_PAD)), const((1, MLA_QK_PAD)),
                  pl.BlockSpec((tm, LANES), lambda bi, i: (i, 0)),
                  pl.BlockSpec((tm, LANES), lambda bi, i: (i, 0)),
                  pl.BlockSpec((tm, LANES), lambda bi, i: (i, 0))],
        out_specs=[pl.BlockSpec((1, MLA_HEADS, tm, MLA_QK_PAD), lambda bi, i: (bi, 0, i, 0)),
                   pl.BlockSpec((1, MLA_HEADS, tm, MLA_QK_PAD), lambda bi, i: (bi, 0, i, 0)),
                   pl.BlockSpec((1, MLA_HEADS, tm, MLA_V), lambda bi, i: (bi, 0, i, 0))],
        out_shape=[jax.ShapeDtypeStruct((b, MLA_HEADS, s, MLA_QK_PAD), jnp.bfloat16),
                   jax.ShapeDtypeStruct((b, MLA_HEADS, s, MLA_QK_PAD), jnp.bfloat16),
                   jax.ShapeDtypeStruct((b, MLA_HEADS, s, MLA_V), jnp.bfloat16)],
        compiler_params=_cparams(("parallel", "parallel")),
        name="mla_prep",
    )(proj3, proj3, proj3, g_cq, g_ckv, wuq_p, wukv_p, gq_pad, gk_pad, rope_c, rope_s1, rope_s2)


def _dsa_prep_kernel(q_ref, k_ref, v_ref, qi_ref, ki_ref, gq_ref, gk_ref,
                     qo_ref, ko_ref, vo_ref, qio_ref, kio_ref):
    gq, gk = gq_ref[...], gk_ref[...]
    scale = DSA_HD ** -0.5
    for h in range(DSA_HEADS):
        sl = slice(h * DSA_HD, (h + 1) * DSA_HD)
        qh = q_ref[0, :, sl]
        qo_ref[0, h] = (qh * lax.rsqrt(jnp.mean(qh * qh, axis=-1, keepdims=True) + EPS) * gq * scale
                        ).astype(jnp.bfloat16)
        kh = k_ref[0, :, sl]
        ko_ref[0, h] = (kh * lax.rsqrt(jnp.mean(kh * kh, axis=-1, keepdims=True) + EPS) * gk
                        ).astype(jnp.bfloat16)
        vo_ref[0, h] = v_ref[0, :, sl].astype(jnp.bfloat16)
    qio_ref[0] = (qi_ref[0] * (IDX_HD ** -0.5)).astype(jnp.bfloat16)
    ki = ki_ref[0]
    kio_ref[0, 0] = ki.astype(jnp.bfloat16)
    kio_ref[0, 1] = pltpu.roll(ki, IDX_HD, 1).astype(jnp.bfloat16)


def _dsa_prep(proj3, gq, gk, tm=256):
    b, s, _ = proj3.shape
    w = DSA_HEADS * DSA_HD
    hd_spec = pl.BlockSpec((1, DSA_HEADS, tm, DSA_HD), lambda bi, i: (bi, 0, i, 0))
    hd_shape = jax.ShapeDtypeStruct((b, DSA_HEADS, s, DSA_HD), jnp.bfloat16)
    return pl.pallas_call(
        _dsa_prep_kernel,
        grid=(b, s // tm),
        in_specs=[pl.BlockSpec((1, tm, w), lambda bi, i: (bi, i, COL_Q // w)),
                  pl.BlockSpec((1, tm, w), lambda bi, i: (bi, i, COL_K // w)),
                  pl.BlockSpec((1, tm, w), lambda bi, i: (bi, i, COL_V // w)),
                  pl.BlockSpec((1, tm, w), lambda bi, i: (bi, i, COL_QI // w)),
                  pl.BlockSpec((1, tm, LANES), lambda bi, i: (bi, i, COL_KI // LANES)),
                  pl.BlockSpec((1, DSA_HD), lambda bi, i: (0, 0)),
                  pl.BlockSpec((1, DSA_HD), lambda bi, i: (0, 0))],
        out_specs=[hd_spec, hd_spec, hd_spec,
                   pl.BlockSpec((1, tm, w), lambda bi, i: (bi, i, 0)),
                   pl.BlockSpec((1, 2, tm, LANES), lambda bi, i: (bi, 0, i, 0))],
        out_shape=[hd_shape, hd_shape, hd_shape,
                   jax.ShapeDtypeStruct((b, s, w), jnp.bfloat16),
                   jax.ShapeDtypeStruct((b, 2, s, LANES), jnp.bfloat16)],
        compiler_params=_cparams(("parallel", "parallel")),
        name="dsa_prep",
    )(proj3, proj3, proj3, proj3, proj3, gq, gk)


def _lane_fold(x, op):
    out = x[:, :LANES]
    for c in range(1, x.shape[1] // LANES):
        out = op(out, x[:, c * LANES:(c + 1) * LANES])
    return out


ATT_SLOTS = 8


def _attn_scratch(nsub):
    t = ATT_T
    return ([pltpu.VMEM((nsub, t, LANES), jnp.float32)] * 3
            + [pltpu.VMEM((ATT_SLOTS, t, t), jnp.float32), pltpu.VMEM((ATT_SLOTS, t, t), jnp.bfloat16)])


def _attn_sweep(q_ref, k_ref, v_ref, o_ref, m_scr, l_scr, acc_scr, s_scr, p_scr, add_fn):
    qi = pl.program_id(2)
    t = ATT_T
    nsub = q_ref.shape[2] // t
    nrep = t // LANES
    nslot = s_scr.shape[0]
    first = nsub * qi

    def qk(slot, unit):
        r, ki, _ = unit
        off = pl.multiple_of(ki * t, t)
        s_scr[slot] = lax.dot_general(q_ref[0, 0, r * t:(r + 1) * t, :], k_ref[0, 0, pl.ds(off, t), :],
                                      _NT, preferred_element_type=jnp.float32)

    def logits(slot, unit):
        a = add_fn(*unit)
        return s_scr[slot] if a is None else s_scr[slot] + a

    def fold_max(slot, unit):
        r = unit[0]
        m_scr[r] = jnp.maximum(m_scr[r], _lane_fold(logits(slot, unit), jnp.maximum))

    def probs(slot, unit):
        r = unit[0]
        p = jnp.exp(logits(slot, unit) - _rep(m_scr[r], nrep))
        l_scr[r] += _lane_fold(p, jnp.add)
        p_scr[slot] = p.astype(jnp.bfloat16)

    def pv(slot, unit):
        r, ki, _ = unit
        off = pl.multiple_of(ki * t, t)
        acc_scr[r] += jnp.dot(p_scr[slot], v_ref[0, 0, pl.ds(off, t), :], preferred_element_type=jnp.float32)

    def run(units, stages):
        n = len(units)
        for step in range(n + len(stages) - 1):
            for k, stage in enumerate(stages):
                i = step - k
                if 0 <= i < n:
                    stage(i % nslot, units[i])

    def sweep(stages):
        def body(kp, carry):
            run([(r, 2 * kp + kk, None) for kk in range(2) for r in range(nsub)], stages)
            return carry

        if nsub % 2 == 0:
            lax.fori_loop(0, first // 2, body, 0)
        else:
            lax.fori_loop(0, first, lambda ki, c: (run([(0, ki, None)], stages), c)[1], 0)
        run([(r, first + j, j - r) for j in range(nsub) for r in range(j, nsub)], stages)

    m_scr[...] = jnp.full(m_scr.shape, NEG, jnp.float32)
    sweep((qk, fold_max))
    for r in range(nsub):
        m_scr[r] = jnp.broadcast_to(jnp.max(m_scr[r], axis=1, keepdims=True), (t, LANES))
    l_scr[...] = jnp.zeros(l_scr.shape, jnp.float32)
    acc_scr[...] = jnp.zeros(acc_scr.shape, jnp.float32)
    sweep((qk, probs, pv))
    for r in range(nsub):
        l = jnp.sum(l_scr[r], axis=1, keepdims=True)
        o_ref[0, r * t:(r + 1) * t, :] = (acc_scr[r] / l).astype(o_ref.dtype)


def _attn_nsub(n_tiles):
    return 4 if n_tiles % 4 == 0 else (2 if n_tiles % 2 == 0 else 1)


def _mla_attn_kernel(q_ref, k_ref, v_ref, mask_ref, o_ref, *scr):
    def add_fn(r, ki, rel):
        return mask_ref[...] if rel == 0 else None

    _attn_sweep(q_ref, k_ref, v_ref, o_ref, *scr, add_fn)


def _mla_attn(q, k, v, mask_tbl):
    b, h, s, dq = q.shape
    t = ATT_T
    nsub = _attn_nsub(s // t)
    tq = nsub * t
    return pl.pallas_call(
        _mla_attn_kernel,
        grid=(b, h, s // tq),
        in_specs=[pl.BlockSpec((1, 1, tq, dq), lambda bi, hi, i: (bi, hi, i, 0)),
                  pl.BlockSpec((1, 1, s, dq), lambda bi, hi, i: (bi, hi, 0, 0)),
                  pl.BlockSpec((1, 1, s, MLA_V), lambda bi, hi, i: (bi, hi, 0, 0)),
                  pl.BlockSpec((t, t), lambda bi, hi, i: (0, 0))],
        out_specs=pl.BlockSpec((1, tq, MLA_V), lambda bi, hi, i: (bi, i, hi)),
        out_shape=jax.ShapeDtypeStruct((b, s, h * MLA_V), jnp.bfloat16),
        scratch_shapes=_attn_scratch(nsub),
        compiler_params=_cparams(("parallel", "parallel", "arbitrary")),
        name="mla_attn",
    )(q, k, v, mask_tbl)


def _dsa_attn_kernel(q_ref, k_ref, v_ref, sel_ref, bias_ref, o_ref, *scr):
    nsub = q_ref.shape[2] // ATT_T
    first = nsub * pl.program_id(2)

    def add_fn(r, ki, rel):
        back = jnp.minimum(first + r - ki, 2) if rel is None else min(-rel, 2)
        return bias_ref[0, back] + sel_ref[0, r, ki].astype(jnp.float32)

    _attn_sweep(q_ref, k_ref, v_ref, o_ref, *scr, add_fn)


def _dsa_attn(q, k, v, sel, bias_tbl):
    b, h, s, d = q.shape
    t = ATT_T
    n = s // t
    nsub = _attn_nsub(n)
    tq = nsub * t
    return pl.pallas_call(
        _dsa_attn_kernel,
        grid=(b, h, s // tq),
        in_specs=[pl.BlockSpec((1, 1, tq, d), lambda bi, hi, i: (bi, hi, i, 0)),
                  pl.BlockSpec((1, 1, s, d), lambda bi, hi, i: (bi, hi, 0, 0)),
                  pl.BlockSpec((1, 1, s, d), lambda bi, hi, i: (bi, hi, 0, 0)),
                  pl.BlockSpec((1, nsub, n, t, t), lambda bi, hi, i: (bi, i, 0, 0, 0)),
                  pl.BlockSpec((1, 3, t, t), lambda bi, hi, i: (hi, 0, 0, 0))],
        out_specs=pl.BlockSpec((1, tq, d), lambda bi, hi, i: (bi, i, hi)),
        out_shape=jax.ShapeDtypeStruct((b, s, h * d), jnp.bfloat16),
        scratch_shapes=_attn_scratch(nsub),
        compiler_params=_cparams(("parallel", "parallel", "arbitrary")),
        name="dsa_attn",
    )(q, k, v, sel, bias_tbl)


def _bias_kernel(rb_ref, o_ref):
    h = pl.program_id(0)
    t = o_ref.shape[2]
    row = lax.broadcasted_iota(jnp.int32, (t, t), 0)
    col = lax.broadcasted_iota(jnp.int32, (t, t), 1)
    nb = REL_BUCKETS // 2
    max_exact = nb // 2
    for r in range(2):
        rel = col - row - r * t
        n = jnp.abs(rel)
        n2 = n * n
        large = jnp.full((t, t), max_exact, jnp.int32)
        for j in range(1, nb - max_exact):
            large = large + (n2 >= (max_exact * max_exact) * (2 ** j)).astype(jnp.int32)
        bucket = jnp.where(rel > 0, nb, 0) + jnp.where(n < max_exact, n, large)
        val = jnp.zeros((t, t), jnp.float32)
        for bkt in range(REL_BUCKETS):
            val = jnp.where(bucket == bkt, rb_ref[bkt, h], val)
        o_ref[0, r] = val
    o_ref[0, 2] = jnp.full((t, t), rb_ref[nb - 1, h], jnp.float32)


def _bias_tiles(rel_bias):
    t = ATT_T
    assert t >= REL_MAX_DIST
    return pl.pallas_call(
        _bias_kernel,
        grid=(DSA_HEADS,),
        in_specs=[pl.BlockSpec(memory_space=pltpu.SMEM)],
        out_specs=pl.BlockSpec((1, 3, t, t), lambda h: (h, 0, 0, 0)),
        out_shape=jax.ShapeDtypeStruct((DSA_HEADS, 3, t, t), jnp.float32),
        compiler_params=_cparams(("arbitrary",)),
        name="t5_bias",
    )(rel_bias)


_F32_INF_BITS = 0x7F800000
_INT_MIN = np.int32(-2 ** 31)


def _index_kernel(n_sel, q_ref, w_ref, k_ref, o_ref, score_scr, wt_scr):
    qi = pl.program_id(1)
    t = q_ref.shape[1]
    n_tiles = score_scr.shape[0]
    n_adm = qi + 1
    wt_scr[...] = w_ref[0].T
    q_chunk = (qi * t + lax.broadcasted_iota(jnp.int32, (1, t), 1)) // CHUNK

    def score_body(c, carry):
        off = pl.multiple_of(c * t, t)
        acc = jnp.zeros((t, t), jnp.float32)
        for hp in range(IDX_HEADS // 2):
            qpair = q_ref[0, :, hp * LANES:(hp + 1) * LANES]
            for par in range(2):
                h = 2 * hp + par
                kk = k_ref[0, par, pl.ds(off, t), :]
                d = lax.dot_general(kk, qpair, _NT, preferred_element_type=jnp.float32)
                acc = acc + jnp.maximum(d, 0.0) * wt_scr[h:h + 1, :]
        k_chunk = (c * t + lax.broadcasted_iota(jnp.int32, (t, 1), 0)) // CHUNK
        score_scr[c] = jnp.where(k_chunk <= q_chunk, acc, -jnp.inf)
        return carry

    lax.fori_loop(0, n_adm, score_body, 0)

    def count_ge(cand):
        def cnt_body(c, cnt):
            ge = jnp.where(score_scr[c] >= cand, 1.0, 0.0)
            return cnt + jnp.sum(ge.reshape(4, t // 32, 8, t), axis=1)

        cnt = lax.fori_loop(0, n_adm, cnt_body, jnp.zeros((4, 8, t), jnp.float32))
        return jnp.sum(jnp.sum(cnt, axis=0), axis=0, keepdims=True)

    kf = float(n_sel)
    pos = count_ge(jnp.zeros((1, t), jnp.float32)) >= kf

    def thr_of(u):
        neg_bits = jnp.bitwise_or(_F32_INF_BITS - u, _INT_MIN)
        return pltpu.bitcast(jnp.where(pos, u, neg_bits), jnp.float32)

    def bit_body(it, u):
        cand = u + jnp.left_shift(jnp.int32(1), 30 - it)
        ok = jnp.logical_and(cand <= _F32_INF_BITS, count_ge(thr_of(cand)) >= kf)
        return jnp.where(ok, cand, u)

    thr = thr_of(lax.fori_loop(0, 31, bit_body, jnp.zeros((1, t), jnp.int32)))

    def out_body(c, carry):
        score = score_scr[c]
        sel = jnp.logical_and(score >= thr, score > -jnp.inf)
        o_ref[0, 0, c] = jnp.where(sel, 0.0, NEG).T.astype(o_ref.dtype)
        return carry

    lax.fori_loop(0, n_adm, out_body, 0)

    def fill_body(c, carry):
        o_ref[0, 0, c] = jnp.full((t, t), NEG, o_ref.dtype)
        return carry

    lax.fori_loop(n_adm, n_tiles, fill_body, 0)


def _dsa_index(q_idx, w3, k_idx2, n_sel, w_col_block):
    b, s, _ = q_idx.shape
    t = ATT_T
    n = s // t
    return pl.pallas_call(
        functools.partial(_index_kernel, n_sel),
        grid=(b, n),
        in_specs=[pl.BlockSpec((1, t, IDX_HEADS * IDX_HD), lambda bi, i: (bi, i, 0)),
                  pl.BlockSpec((1, t, LANES), lambda bi, i: (bi, i, w_col_block)),
                  pl.BlockSpec((1, 2, s, LANES), lambda bi, i: (bi, 0, 0, 0))],
        out_specs=pl.BlockSpec((1, 1, n, t, t), lambda bi, i: (bi, i, 0, 0, 0)),
        out_shape=jax.ShapeDtypeStruct((b, n, n, t, t), jnp.bfloat16),
        scratch_shapes=[pltpu.VMEM((n, t, t), jnp.float32),
                        pltpu.VMEM((LANES, t), jnp.float32)],
        compiler_params=_cparams(("parallel", "parallel")),
        name="dsa_index",
    )(q_idx, w3, k_idx2)


def _outproj_kernel(x_ref, ya_ref, yb_ref, wa_ref, wb_ref, g_ref, x1_ref, h2t_ref):
    y = (jnp.dot(ya_ref[...], wa_ref[...], preferred_element_type=jnp.float32)
         + jnp.dot(yb_ref[...], wb_ref[...], preferred_element_type=jnp.float32))
    x1 = x_ref[...] + y
    x1_ref[...] = x1
    ms = jnp.mean(x1 * x1, axis=-1, keepdims=True)
    h2t_ref[...] = (x1 * lax.rsqrt(ms + EPS) * g_ref[...]).T.astype(jnp.bfloat16)


def _outproj(x2d, ya, yb, wo_a, wo_b, g_ffn, tm=256):
    m, d = x2d.shape
    ka, kb = ya.shape[1], yb.shape[1]
    return pl.pallas_call(
        _outproj_kernel,
        grid=(m // tm,),
        in_specs=[pl.BlockSpec((tm, d), lambda i: (i, 0)),
                  pl.BlockSpec((tm, ka), lambda i: (i, 0)),
                  pl.BlockSpec((tm, kb), lambda i: (i, 0)),
                  pl.BlockSpec((ka, d), lambda i: (0, 0)),
                  pl.BlockSpec((kb, d), lambda i: (0, 0)),
                  pl.BlockSpec((1, d), lambda i: (0, 0))],
        out_specs=[pl.BlockSpec((tm, d), lambda i: (i, 0)),
                   pl.BlockSpec((d, tm), lambda i: (0, i))],
        out_shape=[jax.ShapeDtypeStruct((m, d), jnp.float32),
                   jax.ShapeDtypeStruct((d, m), jnp.bfloat16)],
        compiler_params=_cparams(("parallel",)),
        name="outproj",
    )(x2d, ya, yb, wo_a, wo_b, g_ffn)


def _top_values(cur, k):
    out = []
    for _ in range(k):
        m = jnp.max(cur, axis=0, keepdims=True)
        out.append(m)
        cur = jnp.where(cur == m, -jnp.inf, cur)
    return out


_N_TOP = PEER_TOPK + 1
_CAND_PAIRS = [(a, b) for a in range(_N_TOP) for b in range(_N_TOP) if (a + 1) * (b + 1) <= _N_TOP]


def _peer_score_kernel(h_ref, wq_ref, sk_ref, b_ref, beta_ref, a_ref):
    tm = h_ref.shape[1]
    qr_t = jnp.dot(wq_ref[...], h_ref[...], preferred_element_type=jnp.float32)
    for h in range(PEER_HEADS):
        q0 = qr_t[(2 * h) * PEER_NKEYS:(2 * h + 1) * PEER_NKEYS].astype(jnp.bfloat16)
        q1 = qr_t[(2 * h + 1) * PEER_NKEYS:(2 * h + 2) * PEER_NKEYS].astype(jnp.bfloat16)
        s0 = jnp.dot(sk_ref[2 * h], q0, preferred_element_type=jnp.float32)
        s1 = jnp.dot(sk_ref[2 * h + 1], q1, preferred_element_type=jnp.float32)
        top0 = _top_values(s0, _N_TOP)
        top1 = _top_values(s1, _N_TOP)
        cands = [top0[a] + top1[b] for (a, b) in _CAND_PAIRS]
        pad = (-len(cands)) % 8
        cands += [jnp.full((1, tm), -jnp.inf, jnp.float32)] * pad
        vals = _top_values(jnp.concatenate(cands, axis=0), _N_TOP)
        z = jnp.zeros((1, tm), jnp.float32)
        for v in vals[:PEER_TOPK]:
            z = z + jnp.exp(v - vals[0])
        cut = 0.5 * (vals[PEER_TOPK - 1] + vals[PEER_TOPK])
        b_ref[h] = jnp.exp(s1 - top1[0]).astype(b_ref.dtype)
        beta_ref[h] = jnp.exp((cut - top1[0]) - s0)
        a_ref[h] = jnp.exp(s0 - top0[0]) / z


def _peer_scores(h2t, wpq_t, sk, tm=256):
    d, m = h2t.shape
    nq = wpq_t.shape[0]
    out_spec = pl.BlockSpec((PEER_HEADS, PEER_NKEYS, tm), lambda i: (0, 0, i))
    out_shape = jax.ShapeDtypeStruct((PEER_HEADS, PEER_NKEYS, m), jnp.float32)
    return pl.pallas_call(
        _peer_score_kernel,
        grid=(m // tm,),
        in_specs=[pl.BlockSpec((d, tm), lambda i: (0, i)),
                  pl.BlockSpec((nq, d), lambda i: (0, 0)),
                  pl.BlockSpec(sk.shape, lambda i: (0, 0, 0))],
        out_specs=[out_spec, out_spec, out_spec],
        out_shape=[jax.ShapeDtypeStruct(out_shape.shape, jnp.bfloat16), out_shape, out_shape],
        compiler_params=_cparams(("parallel",)),
        name="peer_scores",
    )(h2t, wpq_t, sk)


def _gelu_exact(x):
    return 0.5 * x * (1.0 + lax.erf(x * (1.0 / math.sqrt(2.0))))


PEER_KC = 256


def _peer_main_step(h_ref, u_ref, vt_ref, b_ref, beta_ref, a_ref, acc_scr, w_scr, act_next, act_cur):
    tm = h_ref.shape[1]
    te = u_ref.shape[0]
    act_next[...] = jnp.dot(u_ref[...], h_ref[...], preferred_element_type=jnp.float32)
    parts = []
    for c in range(te // PEER_KC):
        for il in range(c * PEER_KC // PEER_NKEYS, (c + 1) * PEER_KC // PEER_NKEYS):
            rows = slice(il * PEER_NKEYS, (il + 1) * PEER_NKEYS)
            for lt in range(tm // LANES):
                lanes = slice(lt * LANES, (lt + 1) * LANES)
                g = None
                for h in range(PEER_HEADS):
                    b = b_ref[h, :, lanes]
                    beta = beta_ref[h, il:il + 1, lanes].astype(jnp.bfloat16)
                    a = a_ref[h, il:il + 1, lanes].astype(jnp.bfloat16)
                    term = a * jnp.where(b >= beta, b, jnp.zeros_like(b))
                    g = term if g is None else g + term
                w_scr[rows, lanes] = g * _gelu_exact(act_cur[rows, lanes]).astype(jnp.bfloat16)
        ck = slice(c * PEER_KC, (c + 1) * PEER_KC)
        parts.append(jnp.dot(vt_ref[:, ck], w_scr[ck, :], preferred_element_type=jnp.float32))
    acc_scr[...] += functools.reduce(lambda p, q: p + q, parts)


def _peer_main_kernel(h_ref, u_ref, vt_ref, b_ref, beta_ref, a_ref, x1_ref, o_ref,
                      acc_scr, w_scr, act_a, act_b):
    e = pl.program_id(1)

    @pl.when(e == 0)
    def _():
        acc_scr[...] = jnp.zeros(acc_scr.shape, jnp.float32)
        act_b[...] = jnp.zeros(act_b.shape, jnp.float32)

    args = (h_ref, u_ref, vt_ref, b_ref, beta_ref, a_ref, acc_scr, w_scr)

    @pl.when(e % 2 == 0)
    def _():
        _peer_main_step(*args, act_a, act_b)

    @pl.when(e % 2 == 1)
    def _():
        _peer_main_step(*args, act_b, act_a)

    @pl.when(e == pl.num_programs(1) - 1)
    def _():
        o_ref[...] = x1_ref[...] + acc_scr[...].T


def _peer_main(h2t, u_bf, vt_bf, bs, betas, a_s, x1, tm=512, te=1024):
    d, m = h2t.shape
    ne = u_bf.shape[0] // te
    ni = te // PEER_NKEYS
    back = lambda e, n: jnp.clip(e - n, 0, ne - 1)
    once = pl.Buffered(1)
    return pl.pallas_call(
        _peer_main_kernel,
        grid=(m // tm, ne + 1),
        in_specs=[pl.BlockSpec((d, tm), lambda i, e: (0, i), pipeline_mode=once),
                  pl.BlockSpec((te, d), lambda i, e: (back(e, 0), 0)),
                  pl.BlockSpec((d, te), lambda i, e: (0, back(e, 1))),
                  pl.BlockSpec((PEER_HEADS, PEER_NKEYS, tm), lambda i, e: (0, 0, i), pipeline_mode=once),
                  pl.BlockSpec((PEER_HEADS, ni, tm), lambda i, e: (0, back(e, 1), i)),
                  pl.BlockSpec((PEER_HEADS, ni, tm), lambda i, e: (0, back(e, 1), i)),
                  pl.BlockSpec((tm, d), lambda i, e: (i, 0), pipeline_mode=once)],
        out_specs=pl.BlockSpec((tm, d), lambda i, e: (i, 0)),
        out_shape=jax.ShapeDtypeStruct((m, d), jnp.float32),
        scratch_shapes=[pltpu.VMEM((d, tm), jnp.float32),
                        pltpu.VMEM((te, tm), jnp.bfloat16),
                        pltpu.VMEM((te, tm), jnp.float32),
                        pltpu.VMEM((te, tm), jnp.float32)],
        compiler_params=_cparams(("parallel", "arbitrary")),
        name="peer_main",
    )(h2t, u_bf, vt_bf, bs, betas, a_s, x1)


def _rope_tables(s):
    half = MLA_ROPE // 2
    freqs = ROPE_THETA ** (-2.0 * jnp.arange(half, dtype=jnp.float32) / MLA_ROPE)
    ang = jnp.arange(s, dtype=jnp.float32)[:, None] * freqs[None, :]
    cos, sin = jnp.cos(ang), jnp.sin(ang)
    z = jnp.zeros_like(cos)
    c = jnp.concatenate([cos, cos, z, z], axis=1)
    s1 = jnp.concatenate([z, sin, z, z], axis=1)
    s2 = jnp.concatenate([-sin, z, z, z], axis=1)
    return c, s1, s2


def _mla_mask_table():
    t = ATT_T
    r = np.arange(t)[:, None] // CHUNK
    c = np.arange(t)[None, :] // CHUNK
    return jnp.asarray(np.where(c <= r, 0.0, NEG).astype(np.float32))


def _layer(x, g_attn, w_in, g_cq, g_ckv, w_uq, w_ukv, g_mla_q, g_mla_k, g_dsa_q, g_dsa_k,
           rel_bias, w_o, g_ffn, w_pq, sub_keys, u_tab, v_tab):
    b, s, d = x.shape
    m = b * s
    bf = jnp.bfloat16
    f32 = jnp.float32
    assert s % ATT_T == 0 and m % 512 == 0

    o = np.cumsum([0, Q_LORA, KV_LORA, MLA_ROPE, 1024, 1024, 1024, 1024, IDX_HD, IDX_HEADS])
    zc = lambda n: jnp.zeros((d, n), f32)
    w_in_p = jnp.concatenate([
        w_in[:, o[3]:o[7]], w_in[:, o[0]:o[2]],
        w_in[:, o[2]:o[3]], zc(LANES - MLA_ROPE),
        w_in[:, o[7]:o[8]], zc(LANES - IDX_HD),
        w_in[:, o[8]:o[9]], zc(LANES - IDX_HEADS), zc(IN_PAD - COL_WI - LANES)], axis=1).astype(bf)
    wuq_p = jnp.pad(w_uq.reshape(Q_LORA, MLA_HEADS, MLA_QK),
                    ((0, 0), (0, 0), (0, MLA_QK_PAD - MLA_QK))).reshape(Q_LORA, -1).astype(bf)
    wukv_p = w_ukv.reshape(KV_LORA, MLA_HEADS, 2, MLA_NOPE).transpose(0, 2, 1, 3).reshape(KV_LORA, -1).astype(bf)
    gq_pad = jnp.pad(g_mla_q, (0, MLA_QK_PAD - MLA_QK))[None]
    gk_pad = jnp.pad(g_mla_k, (0, MLA_QK_PAD - MLA_QK))[None]
    rope_c, rope_s1, rope_s2 = _rope_tables(s)

    proj = _inproj(x.reshape(m, d), g_attn[None], w_in_p)
    proj3 = proj.reshape(b, s, IN_PAD)

    q_a, k_a, v_a = _mla_prep(proj3, g_cq[None], g_ckv[None], wuq_p, wukv_p, gq_pad, gk_pad,
                              rope_c, rope_s1, rope_s2)
    y_a = _mla_attn(q_a, k_a, v_a, _mla_mask_table())

    q_d, k_d, v_d, q_i, k_i2 = _dsa_prep(proj3, g_dsa_q[None], g_dsa_k[None])
    n_sel = min(DSA_TOPK, s // 4)
    sel = _dsa_index(q_i, proj3, k_i2, n_sel, COL_WI // LANES)
    y_b = _dsa_attn(q_d, k_d, v_d, sel, _bias_tiles(rel_bias))

    na = MLA_HEADS * MLA_V
    x1, h2t = _outproj(x.reshape(m, d), y_a.reshape(m, na), y_b.reshape(m, -1),
                      w_o[:na].astype(bf), w_o[na:].astype(bf), g_ffn[None])

    sk = sub_keys.reshape(PEER_HEADS * 2, PEER_NKEYS, -1).astype(bf)
    bs, betas, a_s = _peer_scores(h2t, w_pq.T.astype(bf), sk)
    x2 = _peer_main(h2t, u_tab.astype(bf), v_tab.T.astype(bf), bs, betas, a_s, x1)
    return x2.reshape(b, s, d)


def kernel(x, g_attn, w_in, g_cq, g_ckv, w_uq, w_ukv, g_mla_q, g_mla_k, g_dsa_q, g_dsa_k,
           rel_bias, w_o, g_ffn, w_pq, sub_keys, u_tab, v_tab):
    for l in range(w_in.shape[0]):
        x = _layer(x, g_attn[l], w_in[l], g_cq[l], g_ckv[l], w_uq[l], w_ukv[l], g_mla_q[l], g_mla_k[l],
                   g_dsa_q[l], g_dsa_k[l], rel_bias, w_o[l], g_ffn[l], w_pq[l], sub_keys[l],
                   u_tab[l], v_tab[l])
    return x
```

```python
import functools
import math

import numpy as np
import jax
import jax.numpy as jnp
from jax import lax
from jax.experimental import pallas as pl
from jax.experimental.pallas import tpu as pltpu

EPS = 1e-6
CHUNK = 64

MLA_HEADS = 8
MLA_NOPE = 128
MLA_ROPE = 64
MLA_V = 128
MLA_QK = MLA_NOPE + MLA_ROPE
MLA_QK_PAD = 256
Q_LORA = 512
KV_LORA = 256
ROPE_THETA = 10000.0

DSA_HEADS = 8
DSA_HD = 128
IDX_HEADS = 16
IDX_HD = 64
DSA_TOPK = 256

REL_BUCKETS = 32
REL_MAX_DIST = 128

PEER_HEADS = 8
PEER_NKEYS = 128
PEER_TOPK = 16

LANES = 128
NEG = -1e30
VMEM_LIMIT = 56 * 1024 * 1024

COL_Q, COL_K, COL_V, COL_QI = 0, 1024, 2048, 3072
COL_CQ, COL_CKV, COL_KR, COL_KI, COL_WI = 4096, 4608, 4864, 4992, 5120
IN_PAD = 5376

ATT_T = 256

_NT = (((1,), (1,)), ((), ()))


def _cparams(sem):
    return pltpu.CompilerParams(dimension_semantics=sem, vmem_limit_bytes=VMEM_LIMIT)


def _rep(x, n):
    return x if n == 1 else jnp.concatenate([x] * n, axis=1)


def _inproj_kernel(x_ref, g_ref, w_ref, o_ref, h_scr):
    @pl.when(pl.program_id(1) == 0)
    def _():
        x = x_ref[...]
        ms = jnp.mean(x * x, axis=-1, keepdims=True)
        h_scr[...] = (x * lax.rsqrt(ms + EPS) * g_ref[...]).astype(jnp.bfloat16)

    o_ref[...] = jnp.dot(h_scr[...], w_ref[...], preferred_element_type=jnp.float32)


def _inproj(x2d, g, w_p, tm=1024, tn=768):
    m, d = x2d.shape
    tm = min(tm, m)
    n = w_p.shape[1]
    return pl.pallas_call(
        _inproj_kernel,
        grid=(m // tm, n // tn),
        in_specs=[pl.BlockSpec((tm, d), lambda i, j: (i, 0)),
                  pl.BlockSpec((1, d), lambda i, j: (0, 0)),
                  pl.BlockSpec((d, tn), lambda i, j: (0, j))],
        out_specs=pl.BlockSpec((tm, tn), lambda i, j: (i, j)),
        out_shape=jax.ShapeDtypeStruct((m, n), jnp.float32),
        scratch_shapes=[pltpu.VMEM((tm, d), jnp.bfloat16)],
        compiler_params=_cparams(("parallel", "arbitrary")),
        name="inproj",
    )(x2d, g, w_p)


def _rope128(xr, c, s1, s2):
    return xr * c + pltpu.roll(xr, 96, 1) * s2 + pltpu.roll(xr, 32, 1) * s1


def _mla_prep_kernel(cq_ref, ckv_ref, kr_ref, gcq_ref, gckv_ref, wuq_ref, wukv_ref, gq_ref, gk_ref,
                     c_ref, s1_ref, s2_ref, q_ref, k_ref, v_ref):
    cq = cq_ref[0]
    cqn = (cq * lax.rsqrt(jnp.mean(cq * cq, axis=-1, keepdims=True) + EPS) * gcq_ref[...])
    q_all = jnp.dot(cqn.astype(jnp.bfloat16), wuq_ref[...], preferred_element_type=jnp.float32)
    ckv = ckv_ref[0]
    ckvn = (ckv * lax.rsqrt(jnp.mean(ckv * ckv, axis=-1, keepdims=True) + EPS) * gckv_ref[...])
    kv = jnp.dot(ckvn.astype(jnp.bfloat16), wukv_ref[...], preferred_element_type=jnp.float32)
    kr = kr_ref[0]
    kr_ss = jnp.sum(kr * kr, axis=-1, keepdims=True)
    c, s1, s2 = c_ref[...], s1_ref[...], s2_ref[...]
    gq, gk = gq_ref[...], gk_ref[...]
    scale = MLA_QK ** -0.5
    for h in range(MLA_HEADS):
        qh = q_all[:, h * MLA_QK_PAD:(h + 1) * MLA_QK_PAD]
        rq = lax.rsqrt(jnp.sum(qh * qh, axis=-1, keepdims=True) * (1.0 / MLA_QK) + EPS)
        qn = qh * rq * gq
        q_ref[0, h, :, :MLA_NOPE] = (qn[:, :MLA_NOPE] * scale).astype(jnp.bfloat16)
        q_ref[0, h, :, MLA_NOPE:] = (_rope128(qn[:, MLA_NOPE:], c, s1, s2) * scale).astype(jnp.bfloat16)
        kn = kv[:, h * MLA_NOPE:(h + 1) * MLA_NOPE]
        rk = lax.rsqrt((jnp.sum(kn * kn, axis=-1, keepdims=True) + kr_ss) * (1.0 / MLA_QK) + EPS)
        k_ref[0, h, :, :MLA_NOPE] = (kn * rk * gk[:, :MLA_NOPE]).astype(jnp.bfloat16)
        k_ref[0, h, :, MLA_NOPE:] = _rope128(kr * rk * gk[:, MLA_NOPE:], c, s1, s2).astype(jnp.bfloat16)
        v_ref[0, h] = kv[:, MLA_HEADS * MLA_NOPE + h * MLA_V:MLA_HEADS * MLA_NOPE + (h + 1) * MLA_V
                         ].astype(jnp.bfloat16)


def _mla_prep(proj3, g_cq, g_ckv, wuq_p, wukv_p, gq_pad, gk_pad, rope_c, rope_s1, rope_s2, tm=256):
    b, s, _ = proj3.shape
    const = lambda shape: pl.BlockSpec(shape, lambda bi, i: (0,) * len(shape))
    return pl.pallas_call(
        _mla_prep_kernel,
        grid=(b, s // tm),
        in_specs=[pl.BlockSpec((1, tm, Q_LORA), lambda bi, i: (bi, i, COL_CQ // Q_LORA)),
                  pl.BlockSpec((1, tm, KV_LORA), lambda bi, i: (bi, i, COL_CKV // KV_LORA)),
                  pl.BlockSpec((1, tm, LANES), lambda bi, i: (bi, i, COL_KR // LANES)),
                  const((1, Q_LORA)), const((1, KV_LORA)),
                  const(wuq_p.shape), const(wukv_p.shape),
                  const((1, MLA_QK_PAD)), const((1, MLA_QK_PAD)),
                  pl.BlockSpec((tm, LANES), lambda bi, i: (i, 0)),
                  pl.BlockSpec((tm, LANES), lambda bi, i: (i, 0)),
                  pl.BlockSpec((tm, LANES), lambda bi, i: (i, 0))],
        out_specs=[pl.BlockSpec((1, MLA_HEADS, tm, MLA_QK_PAD), lambda bi, i: (bi, 0, i, 0)),
                   pl.BlockSpec((1, MLA_HEADS, tm, MLA_QK_PAD), lambda bi, i: (bi, 0, i, 0)),
                   pl.BlockSpec((1, MLA_HEADS, tm, MLA_V), lambda bi, i: (bi, 0, i, 0))],
        out_shape=[jax.ShapeDtypeStruct((b, MLA_HEADS, s, MLA_QK_PAD), jnp.bfloat16),
                   jax.ShapeDtypeStruct((b, MLA_HEADS, s, MLA_QK_PAD), jnp.bfloat16),
                   jax.ShapeDtypeStruct((b, MLA_HEADS, s, MLA_V), jnp.bfloat16)],
        compiler_params=_cparams(("parallel", "parallel")),
        name="mla_prep",
    )(proj3, proj3, proj3, g_cq, g_ckv, wuq_p, wukv_p, gq_pad, gk_pad, rope_c, rope_s1, rope_s2)


def _dsa_prep_kernel(q_ref, k_ref, v_ref, qi_ref, ki_ref, gq_ref, gk_ref,
                     qo_ref, ko_ref, vo_ref, qio_ref, kio_ref):
    gq, gk = gq_ref[...], gk_ref[...]
    scale = DSA_HD ** -0.5
    for h in range(DSA_HEADS):
        sl = slice(h * DSA_HD, (h + 1) * DSA_HD)
        qh = q_ref[0, :, sl]
        qo_ref[0, h] = (qh * lax.rsqrt(jnp.mean(qh * qh, axis=-1, keepdims=True) + EPS) * gq * scale
                        ).astype(jnp.bfloat16)
        kh = k_ref[0, :, sl]
        ko_ref[0, h] = (kh * lax.rsqrt(jnp.mean(kh * kh, axis=-1, keepdims=True) + EPS) * gk
                        ).astype(jnp.bfloat16)
        vo_ref[0, h] = v_ref[0, :, sl].astype(jnp.bfloat16)
    qio_ref[0] = (qi_ref[0] * (IDX_HD ** -0.5)).astype(jnp.bfloat16)
    ki = ki_ref[0]
    kio_ref[0, 0] = ki.astype(jnp.bfloat16)
    kio_ref[0, 1] = pltpu.roll(ki, IDX_HD, 1).astype(jnp.bfloat16)


def _dsa_prep(proj3, gq, gk, tm=256):
    b, s, _ = proj3.shape
    w = DSA_HEADS * DSA_HD
    hd_spec = pl.BlockSpec((1, DSA_HEADS, tm, DSA_HD), lambda bi, i: (bi, 0, i, 0))
    hd_shape = jax.ShapeDtypeStruct((b, DSA_HEADS, s, DSA_HD), jnp.bfloat16)
    return pl.pallas_call(
        _dsa_prep_kernel,
        grid=(b, s // tm),
        in_specs=[pl.BlockSpec((1, tm, w), lambda bi, i: (bi, i, COL_Q // w)),
                  pl.BlockSpec((1, tm, w), lambda bi, i: (bi, i, COL_K // w)),
                  pl.BlockSpec((1, tm, w), lambda bi, i: (bi, i, COL_V // w)),
                  pl.BlockSpec((1, tm, w), lambda bi, i: (bi, i, COL_QI // w)),
                  pl.BlockSpec((1, tm, LANES), lambda bi, i: (bi, i, COL_KI // LANES)),
                  pl.BlockSpec((1, DSA_HD), lambda bi, i: (0, 0)),
                  pl.BlockSpec((1, DSA_HD), lambda bi, i: (0, 0))],
        out_specs=[hd_spec, hd_spec, hd_spec,
                   pl.BlockSpec((1, tm, w), lambda bi, i: (bi, i, 0)),
                   pl.BlockSpec((1, 2, tm, LANES), lambda bi, i: (bi, 0, i, 0))],
        out_shape=[hd_shape, hd_shape, hd_shape,
                   jax.ShapeDtypeStruct((b, s, w), jnp.bfloat16),
                   jax.ShapeDtypeStruct((b, 2, s, LANES), jnp.bfloat16)],
        compiler_params=_cparams(("parallel", "parallel")),
        name="dsa_prep",
    )(proj3, proj3, proj3, proj3, proj3, gq, gk)


def _lane_fold(x, op):
    out = x[:, :LANES]
    for c in range(1, x.shape[1] // LANES):
        out = op(out, x[:, c * LANES:(c + 1) * LANES])
    return out


ATT_SLOTS = 8


def _attn_scratch(nsub, n_tiles):
    t = ATT_T
    return ([pltpu.VMEM((nsub, t, LANES), jnp.float32)] * 3
            + [pltpu.VMEM((ATT_SLOTS, t, t), jnp.float32), pltpu.VMEM((ATT_SLOTS, t, t), jnp.bfloat16),
               pltpu.VMEM((nsub * n_tiles, t, t), jnp.float32)])


def _attn_sweep(q_ref, k_ref, v_ref, o_ref, m_scr, l_scr, acc_scr, s_scr, p_scr, all_scr, add_fn):
    qi = pl.program_id(2)
    t = ATT_T
    nsub = q_ref.shape[2] // t
    nrep = t // LANES
    nslot = s_scr.shape[0]
    first = nsub * qi

    def qk(slot, unit):
        r, ki, _ = unit
        off = pl.multiple_of(ki * t, t)
        s_scr[slot] = lax.dot_general(q_ref[0, 0, r * t:(r + 1) * t, :], k_ref[0, 0, pl.ds(off, t), :],
                                      _NT, preferred_element_type=jnp.float32)

    n_tiles = all_scr.shape[0] // nsub

    def fold_max(slot, unit):
        r, ki, _ = unit
        a = add_fn(*unit)
        s = s_scr[slot] if a is None else s_scr[slot] + a
        all_scr[r * n_tiles + ki] = s
        m_scr[r] = jnp.maximum(m_scr[r], _lane_fold(s, jnp.maximum))

    def probs(slot, unit):
        r, ki, _ = unit
        p = jnp.exp(all_scr[r * n_tiles + ki] - _rep(m_scr[r], nrep))
        l_scr[r] += _lane_fold(p, jnp.add)
        p_scr[slot] = p.astype(jnp.bfloat16)

    def pv(slot, unit):
        r, ki, _ = unit
        off = pl.multiple_of(ki * t, t)
        acc_scr[r] += jnp.dot(p_scr[slot], v_ref[0, 0, pl.ds(off, t), :], preferred_element_type=jnp.float32)

    def run(units, stages):
        n = len(units)
        for step in range(n + len(stages) - 1):
            for k, stage in enumerate(stages):
                i = step - k
                if 0 <= i < n:
                    stage(i % nslot, units[i])

    def sweep(stages):
        def body(kp, carry):
            run([(r, 2 * kp + kk, None) for kk in range(2) for r in range(nsub)], stages)
            return carry

        if nsub % 2 == 0:
            lax.fori_loop(0, first // 2, body, 0)
        else:
            lax.fori_loop(0, first, lambda ki, c: (run([(0, ki, None)], stages), c)[1], 0)
        run([(r, first + j, j - r) for j in range(nsub) for r in range(j, nsub)], stages)

    m_scr[...] = jnp.full(m_scr.shape, NEG, jnp.float32)
    sweep((qk, fold_max))
    for r in range(nsub):
        m_scr[r] = jnp.broadcast_to(jnp.max(m_scr[r], axis=1, keepdims=True), (t, LANES))
    l_scr[...] = jnp.zeros(l_scr.shape, jnp.float32)
    acc_scr[...] = jnp.zeros(acc_scr.shape, jnp.float32)
    sweep((probs, pv))
    for r in range(nsub):
        l = jnp.sum(l_scr[r], axis=1, keepdims=True)
        o_ref[0, r * t:(r + 1) * t, :] = (acc_scr[r] / l).astype(o_ref.dtype)


def _attn_nsub(n_tiles):
    return 4 if n_tiles % 4 == 0 else (2 if n_tiles % 2 == 0 else 1)


def _mla_attn_kernel(q_ref, k_ref, v_ref, mask_ref, o_ref, *scr):
    def add_fn(r, ki, rel):
        return mask_ref[...] if rel == 0 else None

    _attn_sweep(q_ref, k_ref, v_ref, o_ref, *scr, add_fn)


def _mla_attn(q, k, v, mask_tbl):
    b, h, s, dq = q.shape
    t = ATT_T
    nsub = _attn_nsub(s // t)
    tq = nsub * t
    return pl.pallas_call(
        _mla_attn_kernel,
        grid=(b, h, s // tq),
        in_specs=[pl.BlockSpec((1, 1, tq, dq), lambda bi, hi, i: (bi, hi, i, 0)),
                  pl.BlockSpec((1, 1, s, dq), lambda bi, hi, i: (bi, hi, 0, 0)),
                  pl.BlockSpec((1, 1, s, MLA_V), lambda bi, hi, i: (bi, hi, 0, 0)),
                  pl.BlockSpec((t, t), lambda bi, hi, i: (0, 0))],
        out_specs=pl.BlockSpec((1, tq, MLA_V), lambda bi, hi, i: (bi, i, hi)),
        out_shape=jax.ShapeDtypeStruct((b, s, h * MLA_V), jnp.bfloat16),
        scratch_shapes=_attn_scratch(nsub, s // t),
        compiler_params=_cparams(("parallel", "parallel", "arbitrary")),
        name="mla_attn",
    )(q, k, v, mask_tbl)


def _dsa_attn_kernel(q_ref, k_ref, v_ref, sel_ref, bias_ref, o_ref, *scr):
    nsub = q_ref.shape[2] // ATT_T
    first = nsub * pl.program_id(2)

    def add_fn(r, ki, rel):
        back = jnp.minimum(first + r - ki, 2) if rel is None else min(-rel, 2)
        return bias_ref[0, back] + sel_ref[0, r, ki].astype(jnp.float32)

    _attn_sweep(q_ref, k_ref, v_ref, o_ref, *scr, add_fn)


def _dsa_attn(q, k, v, sel, bias_tbl):
    b, h, s, d = q.shape
    t = ATT_T
    n = s // t
    nsub = _attn_nsub(n)
    tq = nsub * t
    return pl.pallas_call(
        _dsa_attn_kernel,
        grid=(b, h, s // tq),
        in_specs=[pl.BlockSpec((1, 1, tq, d), lambda bi, hi, i: (bi, hi, i, 0)),
                  pl.BlockSpec((1, 1, s, d), lambda bi, hi, i: (bi, hi, 0, 0)),
                  pl.BlockSpec((1, 1, s, d), lambda bi, hi, i: (bi, hi, 0, 0)),
                  pl.BlockSpec((1, nsub, n, t, t), lambda bi, hi, i: (bi, i, 0, 0, 0)),
                  pl.BlockSpec((1, 3, t, t), lambda bi, hi, i: (hi, 0, 0, 0))],
        out_specs=pl.BlockSpec((1, tq, d), lambda bi, hi, i: (bi, i, hi)),
        out_shape=jax.ShapeDtypeStruct((b, s, h * d), jnp.bfloat16),
        scratch_shapes=_attn_scratch(nsub, s // t),
        compiler_params=_cparams(("parallel", "parallel", "arbitrary")),
        name="dsa_attn",
    )(q, k, v, sel, bias_tbl)


def _bias_kernel(rb_ref, o_ref):
    h = pl.program_id(0)
    t = o_ref.shape[2]
    row = lax.broadcasted_iota(jnp.int32, (t, t), 0)
    col = lax.broadcasted_iota(jnp.int32, (t, t), 1)
    nb = REL_BUCKETS // 2
    max_exact = nb // 2
    for r in range(2):
        rel = col - row - r * t
        n = jnp.abs(rel)
        n2 = n * n
        large = jnp.full((t, t), max_exact, jnp.int32)
        for j in range(1, nb - max_exact):
            large = large + (n2 >= (max_exact * max_exact) * (2 ** j)).astype(jnp.int32)
        bucket = jnp.where(rel > 0, nb, 0) + jnp.where(n < max_exact, n, large)
        val = jnp.zeros((t, t), jnp.float32)
        for bkt in range(REL_BUCKETS):
            val = jnp.where(bucket == bkt, rb_ref[bkt, h], val)
        o_ref[0, r] = val
    o_ref[0, 2] = jnp.full((t, t), rb_ref[nb - 1, h], jnp.float32)


def _bias_tiles(rel_bias):
    t = ATT_T
    assert t >= REL_MAX_DIST
    return pl.pallas_call(
        _bias_kernel,
        grid=(DSA_HEADS,),
        in_specs=[pl.BlockSpec(memory_space=pltpu.SMEM)],
        out_specs=pl.BlockSpec((1, 3, t, t), lambda h: (h, 0, 0, 0)),
        out_shape=jax.ShapeDtypeStruct((DSA_HEADS, 3, t, t), jnp.float32),
        compiler_params=_cparams(("arbitrary",)),
        name="t5_bias",
    )(rel_bias)


_F32_INF_BITS = 0x7F800000
_INT_MIN = np.int32(-2 ** 31)


def _index_kernel(n_sel, q_ref, w_ref, k_ref, o_ref, score_scr, wt_scr):
    qi = pl.program_id(1)
    t = q_ref.shape[1]
    n_tiles = score_scr.shape[0]
    n_adm = qi + 1
    wt_scr[...] = w_ref[0].T
    q_chunk = (qi * t + lax.broadcasted_iota(jnp.int32, (1, t), 1)) // CHUNK

    def score_body(c, carry):
        off = pl.multiple_of(c * t, t)
        acc = jnp.zeros((t, t), jnp.float32)
        for hp in range(IDX_HEADS // 2):
            qpair = q_ref[0, :, hp * LANES:(hp + 1) * LANES]
            for par in range(2):
                h = 2 * hp + par
                kk = k_ref[0, par, pl.ds(off, t), :]
                d = lax.dot_general(kk, qpair, _NT, preferred_element_type=jnp.float32)
                acc = acc + jnp.maximum(d, 0.0) * wt_scr[h:h + 1, :]
        k_chunk = (c * t + lax.broadcasted_iota(jnp.int32, (t, 1), 0)) // CHUNK
        score_scr[c] = jnp.where(k_chunk <= q_chunk, acc, -jnp.inf)
        return carry

    lax.fori_loop(0, n_adm, score_body, 0)

    def count_ge(cand):
        def cnt_body(c, cnt):
            ge = jnp.where(score_scr[c] >= cand, 1.0, 0.0)
            return cnt + jnp.sum(ge.reshape(4, t // 32, 8, t), axis=1)

        cnt = lax.fori_loop(0, n_adm, cnt_body, jnp.zeros((4, 8, t), jnp.float32))
        return jnp.sum(jnp.sum(cnt, axis=0), axis=0, keepdims=True)

    kf = float(n_sel)
    pos = count_ge(jnp.zeros((1, t), jnp.float32)) >= kf

    def thr_of(u):
        neg_bits = jnp.bitwise_or(_F32_INF_BITS - u, _INT_MIN)
        return pltpu.bitcast(jnp.where(pos, u, neg_bits), jnp.float32)

    def bit_body(it, u):
        cand = u + jnp.left_shift(jnp.int32(1), 30 - it)
        ok = jnp.logical_and(cand <= _F32_INF_BITS, count_ge(thr_of(cand)) >= kf)
        return jnp.where(ok, cand, u)

    thr = thr_of(lax.fori_loop(0, 31, bit_body, jnp.zeros((1, t), jnp.int32)))

    def out_body(c, carry):
        score = score_scr[c]
        sel = jnp.logical_and(score >= thr, score > -jnp.inf)
        o_ref[0, 0, c] = jnp.where(sel, 0.0, NEG).T.astype(o_ref.dtype)
        return carry

    lax.fori_loop(0, n_adm, out_body, 0)

    def fill_body(c, carry):
        o_ref[0, 0, c] = jnp.full((t, t), NEG, o_ref.dtype)
        return carry

    lax.fori_loop(n_adm, n_tiles, fill_body, 0)


def _dsa_index(q_idx, w3, k_idx2, n_sel, w_col_block):
    b, s, _ = q_idx.shape
    t = ATT_T
    n = s // t
    return pl.pallas_call(
        functools.partial(_index_kernel, n_sel),
        grid=(b, n),
        in_specs=[pl.BlockSpec((1, t, IDX_HEADS * IDX_HD), lambda bi, i: (bi, i, 0)),
                  pl.BlockSpec((1, t, LANES), lambda bi, i: (bi, i, w_col_block)),
                  pl.BlockSpec((1, 2, s, LANES), lambda bi, i: (bi, 0, 0, 0))],
        out_specs=pl.BlockSpec((1, 1, n, t, t), lambda bi, i: (bi, i, 0, 0, 0)),
        out_shape=jax.ShapeDtypeStruct((b, n, n, t, t), jnp.bfloat16),
        scratch_shapes=[pltpu.VMEM((n, t, t), jnp.float32),
                        pltpu.VMEM((LANES, t), jnp.float32)],
        compiler_params=_cparams(("parallel", "parallel")),
        name="dsa_index",
    )(q_idx, w3, k_idx2)


def _outproj_kernel(x_ref, ya_ref, yb_ref, wa_ref, wb_ref, g_ref, x1_ref, h2t_ref):
    y = (jnp.dot(ya_ref[...], wa_ref[...], preferred_element_type=jnp.float32)
         + jnp.dot(yb_ref[...], wb_ref[...], preferred_element_type=jnp.float32))
    x1 = x_ref[...] + y
    x1_ref[...] = x1
    ms = jnp.mean(x1 * x1, axis=-1, keepdims=True)
    h2t_ref[...] = (x1 * lax.rsqrt(ms + EPS) * g_ref[...]).T.astype(jnp.bfloat16)


def _outproj(x2d, ya, yb, wo_a, wo_b, g_ffn, tm=256):
    m, d = x2d.shape
    ka, kb = ya.shape[1], yb.shape[1]
    return pl.pallas_call(
        _outproj_kernel,
        grid=(m // tm,),
        in_specs=[pl.BlockSpec((tm, d), lambda i: (i, 0)),
                  pl.BlockSpec((tm, ka), lambda i: (i, 0)),
                  pl.BlockSpec((tm, kb), lambda i: (i, 0)),
                  pl.BlockSpec((ka, d), lambda i: (0, 0)),
                  pl.BlockSpec((kb, d), lambda i: (0, 0)),
                  pl.BlockSpec((1, d), lambda i: (0, 0))],
        out_specs=[pl.BlockSpec((tm, d), lambda i: (i, 0)),
                   pl.BlockSpec((d, tm), lambda i: (0, i))],
        out_shape=[jax.ShapeDtypeStruct((m, d), jnp.float32),
                   jax.ShapeDtypeStruct((d, m), jnp.bfloat16)],
        compiler_params=_cparams(("parallel",)),
        name="outproj",
    )(x2d, ya, yb, wo_a, wo_b, g_ffn)


def _top_values(cur, k):
    out = []
    for _ in range(k):
        m = jnp.max(cur, axis=0, keepdims=True)
        out.append(m)
        cur = jnp.where(cur == m, -jnp.inf, cur)
    return out


_N_TOP = PEER_TOPK + 1
_CAND_PAIRS = [(a, b) for a in range(_N_TOP) for b in range(_N_TOP) if (a + 1) * (b + 1) <= _N_TOP]


def _peer_score_kernel(h_ref, wq_ref, sk_ref, b_ref, beta_ref, a_ref):
    tm = h_ref.shape[1]
    qr_t = jnp.dot(wq_ref[...], h_ref[...], preferred_element_type=jnp.float32)
    for h in range(PEER_HEADS):
        q0 = qr_t[(2 * h) * PEER_NKEYS:(2 * h + 1) * PEER_NKEYS].astype(jnp.bfloat16)
        q1 = qr_t[(2 * h + 1) * PEER_NKEYS:(2 * h + 2) * PEER_NKEYS].astype(jnp.bfloat16)
        s0 = jnp.dot(sk_ref[2 * h], q0, preferred_element_type=jnp.float32)
        s1 = jnp.dot(sk_ref[2 * h + 1], q1, preferred_element_type=jnp.float32)
        top0 = _top_values(s0, _N_TOP)
        top1 = _top_values(s1, _N_TOP)
        cands = [top0[a] + top1[b] for (a, b) in _CAND_PAIRS]
        pad = (-len(cands)) % 8
        cands += [jnp.full((1, tm), -jnp.inf, jnp.float32)] * pad
        vals = _top_values(jnp.concatenate(cands, axis=0), _N_TOP)
        z = jnp.zeros((1, tm), jnp.float32)
        for v in vals[:PEER_TOPK]:
            z = z + jnp.exp(v - vals[0])
        cut = 0.5 * (vals[PEER_TOPK - 1] + vals[PEER_TOPK])
        b_ref[h] = jnp.exp(s1 - top1[0]).astype(b_ref.dtype)
        beta_ref[h] = jnp.exp((cut - top1[0]) - s0)
        a_ref[h] = jnp.exp(s0 - top0[0]) / z


def _peer_scores(h2t, wpq_t, sk, tm=256):
    d, m = h2t.shape
    nq = wpq_t.shape[0]
    out_spec = pl.BlockSpec((PEER_HEADS, PEER_NKEYS, tm), lambda i: (0, 0, i))
    out_shape = jax.ShapeDtypeStruct((PEER_HEADS, PEER_NKEYS, m), jnp.float32)
    return pl.pallas_call(
        _peer_score_kernel,
        grid=(m // tm,),
        in_specs=[pl.BlockSpec((d, tm), lambda i: (0, i)),
                  pl.BlockSpec((nq, d), lambda i: (0, 0)),
                  pl.BlockSpec(sk.shape, lambda i: (0, 0, 0))],
        out_specs=[out_spec, out_spec, out_spec],
        out_shape=[jax.ShapeDtypeStruct(out_shape.shape, jnp.bfloat16), out_shape, out_shape],
        compiler_params=_cparams(("parallel",)),
        name="peer_scores",
    )(h2t, wpq_t, sk)


def _gelu_exact(x):
    return 0.5 * x * (1.0 + lax.erf(x * (1.0 / math.sqrt(2.0))))


PEER_KC = 256


def _peer_main_step(h_ref, u_ref, vt_ref, b_ref, beta_ref, a_ref, acc_scr, w_scr, act_next, act_cur):
    tm = h_ref.shape[1]
    te = u_ref.shape[0]
    act_next[...] = jnp.dot(u_ref[...], h_ref[...], preferred_element_type=jnp.float32)
    parts = []
    for c in range(te // PEER_KC):
        for il in range(c * PEER_KC // PEER_NKEYS, (c + 1) * PEER_KC // PEER_NKEYS):
            rows = slice(il * PEER_NKEYS, (il + 1) * PEER_NKEYS)
            for lt in range(tm // LANES):
                lanes = slice(lt * LANES, (lt + 1) * LANES)
                g = None
                for h in range(PEER_HEADS):
                    b = b_ref[h, :, lanes]
                    beta = beta_ref[h, il:il + 1, lanes].astype(jnp.bfloat16)
                    a = a_ref[h, il:il + 1, lanes].astype(jnp.bfloat16)
                    term = a * jnp.where(b >= beta, b, jnp.zeros_like(b))
                    g = term if g is None else g + term
                w_scr[rows, lanes] = g * _gelu_exact(act_cur[rows, lanes]).astype(jnp.bfloat16)
        ck = slice(c * PEER_KC, (c + 1) * PEER_KC)
        parts.append(jnp.dot(vt_ref[:, ck], w_scr[ck, :], preferred_element_type=jnp.float32))
    acc_scr[...] += functools.reduce(lambda p, q: p + q, parts)


def _peer_main_kernel(h_ref, u_ref, vt_ref, b_ref, beta_ref, a_ref, x1_ref, o_ref,
                      acc_scr, w_scr, act_a, act_b):
    e = pl.program_id(1)

    @pl.when(e == 0)
    def _():
        acc_scr[...] = jnp.zeros(acc_scr.shape, jnp.float32)
        act_b[...] = jnp.zeros(act_b.shape, jnp.float32)

    args = (h_ref, u_ref, vt_ref, b_ref, beta_ref, a_ref, acc_scr, w_scr)

    @pl.when(e % 2 == 0)
    def _():
        _peer_main_step(*args, act_a, act_b)

    @pl.when(e % 2 == 1)
    def _():
        _peer_main_step(*args, act_b, act_a)

    @pl.when(e == pl.num_programs(1) - 1)
    def _():
        o_ref[...] = x1_ref[...] + acc_scr[...].T


def _peer_main(h2t, u_bf, vt_bf, bs, betas, a_s, x1, tm=512, te=1024):
    d, m = h2t.shape
    ne = u_bf.shape[0] // te
    ni = te // PEER_NKEYS
    back = lambda e, n: jnp.clip(e - n, 0, ne - 1)
    once = pl.Buffered(1)
    return pl.pallas_call(
        _peer_main_kernel,
        grid=(m // tm, ne + 1),
        in_specs=[pl.BlockSpec((d, tm), lambda i, e: (0, i), pipeline_mode=once),
                  pl.BlockSpec((te, d), lambda i, e: (back(e, 0), 0)),
                  pl.BlockSpec((d, te), lambda i, e: (0, back(e, 1))),
                  pl.BlockSpec((PEER_HEADS, PEER_NKEYS, tm), lambda i, e: (0, 0, i), pipeline_mode=once),
                  pl.BlockSpec((PEER_HEADS, ni, tm), lambda i, e: (0, back(e, 1), i)),
                  pl.BlockSpec((PEER_HEADS, ni, tm), lambda i, e: (0, back(e, 1), i)),
                  pl.BlockSpec((tm, d), lambda i, e: (i, 0), pipeline_mode=once)],
        out_specs=pl.BlockSpec((tm, d), lambda i, e: (i, 0)),
        out_shape=jax.ShapeDtypeStruct((m, d), jnp.float32),
        scratch_shapes=[pltpu.VMEM((d, tm), jnp.float32),
                        pltpu.VMEM((te, tm), jnp.bfloat16),
                        pltpu.VMEM((te, tm), jnp.float32),
                        pltpu.VMEM((te, tm), jnp.float32)],
        compiler_params=_cparams(("parallel", "arbitrary")),
        name="peer_main",
    )(h2t, u_bf, vt_bf, bs, betas, a_s, x1)


def _rope_tables(s):
    half = MLA_ROPE // 2
    freqs = ROPE_THETA ** (-2.0 * jnp.arange(half, dtype=jnp.float32) / MLA_ROPE)
    ang = jnp.arange(s, dtype=jnp.float32)[:, None] * freqs[None, :]
    cos, sin = jnp.cos(ang), jnp.sin(ang)
    z = jnp.zeros_like(cos)
    c = jnp.concatenate([cos, cos, z, z], axis=1)
    s1 = jnp.concatenate([z, sin, z, z], axis=1)
    s2 = jnp.concatenate([-sin, z, z, z], axis=1)
    return c, s1, s2


def _mla_mask_table():
    t = ATT_T
    r = np.arange(t)[:, None] // CHUNK
    c = np.arange(t)[None, :] // CHUNK
    return jnp.asarray(np.where(c <= r, 0.0, NEG).astype(np.float32))


def _layer(x, g_attn, w_in, g_cq, g_ckv, w_uq, w_ukv, g_mla_q, g_mla_k, g_dsa_q, g_dsa_k,
           rel_bias, w_o, g_ffn, w_pq, sub_keys, u_tab, v_tab):
    b, s, d = x.shape
    m = b * s
    bf = jnp.bfloat16
    f32 = jnp.float32
    assert s % ATT_T == 0 and m % 512 == 0

    o = np.cumsum([0, Q_LORA, KV_LORA, MLA_ROPE, 1024, 1024, 1024, 1024, IDX_HD, IDX_HEADS])
    zc = lambda n: jnp.zeros((d, n), f32)
    w_in_p = jnp.concatenate([
        w_in[:, o[3]:o[7]], w_in[:, o[0]:o[2]],
        w_in[:, o[2]:o[3]], zc(LANES - MLA_ROPE),
        w_in[:, o[7]:o[8]], zc(LANES - IDX_HD),
        w_in[:, o[8]:o[9]], zc(LANES - IDX_HEADS), zc(IN_PAD - COL_WI - LANES)], axis=1).astype(bf)
    wuq_p = jnp.pad(w_uq.reshape(Q_LORA, MLA_HEADS, MLA_QK),
                    ((0, 0), (0, 0), (0, MLA_QK_PAD - MLA_QK))).reshape(Q_LORA, -1).astype(bf)
    wukv_p = w_ukv.reshape(KV_LORA, MLA_HEADS, 2, MLA_NOPE).transpose(0, 2, 1, 3).reshape(KV_LORA, -1).astype(bf)
    gq_pad = jnp.pad(g_mla_q, (0, MLA_QK_PAD - MLA_QK))[None]
    gk_pad = jnp.pad(g_mla_k, (0, MLA_QK_PAD - MLA_QK))[None]
    rope_c, rope_s1, rope_s2 = _rope_tables(s)

    proj = _inproj(x.reshape(m, d), g_attn[None], w_in_p)
    proj3 = proj.reshape(b, s, IN_PAD)

    q_a, k_a, v_a = _mla_prep(proj3, g_cq[None], g_ckv[None], wuq_p, wukv_p, gq_pad, gk_pad,
                              rope_c, rope_s1, rope_s2)
    y_a = _mla_attn(q_a, k_a, v_a, _mla_mask_table())

    q_d, k_d, v_d, q_i, k_i2 = _dsa_prep(proj3, g_dsa_q[None], g_dsa_k[None])
    n_sel = min(DSA_TOPK, s // 4)
    sel = _dsa_index(q_i, proj3, k_i2, n_sel, COL_WI // LANES)
    y_b = _dsa_attn(q_d, k_d, v_d, sel, _bias_tiles(rel_bias))

    na = MLA_HEADS * MLA_V
    x1, h2t = _outproj(x.reshape(m, d), y_a.reshape(m, na), y_b.reshape(m, -1),
                      w_o[:na].astype(bf), w_o[na:].astype(bf), g_ffn[None])

    sk = sub_keys.reshape(PEER_HEADS * 2, PEER_NKEYS, -1).astype(bf)
    bs, betas, a_s = _peer_scores(h2t, w_pq.T.astype(bf), sk)
    x2 = _peer_main(h2t, u_tab.astype(bf), v_tab.T.astype(bf), bs, betas, a_s, x1)
    return x2.reshape(b, s, d)


def kernel(x, g_attn, w_in, g_cq, g_ckv, w_uq, w_ukv, g_mla_q, g_mla_k, g_dsa_q, g_dsa_k,
           rel_bias, w_o, g_ffn, w_pq, sub_keys, u_tab, v_tab):
    for l in range(w_in.shape[0]):
        x = _layer(x, g_attn[l], w_in[l], g_cq[l], g_ckv[l], w_uq[l], w_ukv[l], g_mla_q[l], g_mla_k[l],
                   g_dsa_q[l], g_dsa_k[l], rel_bias, w_o[l], g_ffn[l], w_pq[l], sub_keys[l],
                   u_tab[l], v_tab[l])
    return x
```

```python
import functools
import math

import numpy as np
import jax
import jax.numpy as jnp
from jax import lax
from jax.experimental import pallas as pl
from jax.experimental.pallas import tpu as pltpu

EPS = 1e-6
CHUNK = 64

MLA_HEADS = 8
MLA_NOPE = 128
MLA_ROPE = 64
MLA_V = 128
MLA_QK = MLA_NOPE + MLA_ROPE
MLA_QK_PAD = 256
Q_LORA = 512
KV_LORA = 256
ROPE_THETA = 10000.0

DSA_HEADS = 8
DSA_HD = 128
IDX_HEADS = 16
IDX_HD = 64
DSA_TOPK = 256

REL_BUCKETS = 32
REL_MAX_DIST = 128

PEER_HEADS = 8
PEER_NKEYS = 128
PEER_TOPK = 16

LANES = 128
NEG = -1e30
VMEM_LIMIT = 56 * 1024 * 1024

COL_Q, COL_K, COL_V, COL_QI = 0, 1024, 2048, 3072
COL_CQ, COL_CKV, COL_KR, COL_KI, COL_WI = 4096, 4608, 4864, 4992, 5120
IN_PAD = 5376

ATT_T = 256

_NT = (((1,), (1,)), ((), ()))


def _cparams(sem):
    return pltpu.CompilerParams(dimension_semantics=sem, vmem_limit_bytes=VMEM_LIMIT)


def _rep(x, n):
    return x if n == 1 else jnp.concatenate([x] * n, axis=1)


def _inproj_kernel(x_ref, g_ref, w_ref, o_ref, h_scr):
    @pl.when(pl.program_id(1) == 0)
    def _():
        x = x_ref[...]
        ms = jnp.mean(x * x, axis=-1, keepdims=True)
        h_scr[...] = (x * lax.rsqrt(ms + EPS) * g_ref[...]).astype(jnp.bfloat16)

    o_ref[...] = jnp.dot(h_scr[...], w_ref[...], preferred_element_type=jnp.float32)


def _inproj(x2d, g, w_p, tm=1024, tn=768):
    m, d = x2d.shape
    tm = min(tm, m)
    n = w_p.shape[1]
    return pl.pallas_call(
        _inproj_kernel,
        grid=(m // tm, n // tn),
        in_specs=[pl.BlockSpec((tm, d), lambda i, j: (i, 0)),
                  pl.BlockSpec((1, d), lambda i, j: (0, 0)),
                  pl.BlockSpec((d, tn), lambda i, j: (0, j))],
        out_specs=pl.BlockSpec((tm, tn), lambda i, j: (i, j)),
        out_shape=jax.ShapeDtypeStruct((m, n), jnp.float32),
        scratch_shapes=[pltpu.VMEM((tm, d), jnp.bfloat16)],
        compiler_params=_cparams(("parallel", "arbitrary")),
        name="inproj",
    )(x2d, g, w_p)


def _rope128(xr, c, s1, s2):
    return xr * c + pltpu.roll(xr, 96, 1) * s2 + pltpu.roll(xr, 32, 1) * s1


def _mla_prep_kernel(cq_ref, ckv_ref, kr_ref, gcq_ref, gckv_ref, wuq_ref, wukv_ref, gq_ref, gk_ref,
                     c_ref, s1_ref, s2_ref, q_ref, k_ref, v_ref):
    cq = cq_ref[0]
    cqn = (cq * lax.rsqrt(jnp.mean(cq * cq, axis=-1, keepdims=True) + EPS) * gcq_ref[...])
    q_all = jnp.dot(cqn.astype(jnp.bfloat16), wuq_ref[...], preferred_element_type=jnp.float32)
    ckv = ckv_ref[0]
    ckvn = (ckv * lax.rsqrt(jnp.mean(ckv * ckv, axis=-1, keepdims=True) + EPS) * gckv_ref[...])
    kv = jnp.dot(ckvn.astype(jnp.bfloat16), wukv_ref[...], preferred_element_type=jnp.float32)
    kr = kr_ref[0]
    kr_ss = jnp.sum(kr * kr, axis=-1, keepdims=True)
    c, s1, s2 = c_ref[...], s1_ref[...], s2_ref[...]
    gq, gk = gq_ref[...], gk_ref[...]
    scale = MLA_QK ** -0.5
    for h in range(MLA_HEADS):
        qh = q_all[:, h * MLA_QK_PAD:(h + 1) * MLA_QK_PAD]
        rq = lax.rsqrt(jnp.sum(qh * qh, axis=-1, keepdims=True) * (1.0 / MLA_QK) + EPS)
        qn = qh * rq * gq
        q_ref[0, h, :, :MLA_NOPE] = (qn[:, :MLA_NOPE] * scale).astype(jnp.bfloat16)
        q_ref[0, h, :, MLA_NOPE:] = (_rope128(qn[:, MLA_NOPE:], c, s1, s2) * scale).astype(jnp.bfloat16)
        kn = kv[:, h * MLA_NOPE:(h + 1) * MLA_NOPE]
        rk = lax.rsqrt((jnp.sum(kn * kn, axis=-1, keepdims=True) + kr_ss) * (1.0 / MLA_QK) + EPS)
        k_ref[0, h, :, :MLA_NOPE] = (kn * rk * gk[:, :MLA_NOPE]).astype(jnp.bfloat16)
        k_ref[0, h, :, MLA_NOPE:] = _rope128(kr * rk * gk[:, MLA_NOPE:], c, s1, s2).astype(jnp.bfloat16)
        v_ref[0, h] = kv[:, MLA_HEADS * MLA_NOPE + h * MLA_V:MLA_HEADS * MLA_NOPE + (h + 1) * MLA_V
                         ].astype(jnp.bfloat16)


def _mla_prep(proj3, g_cq, g_ckv, wuq_p, wukv_p, gq_pad, gk_pad, rope_c, rope_s1, rope_s2, tm=256):
    b, s, _ = proj3.shape
    const = lambda shape: pl.BlockSpec(shape, lambda bi, i: (0,) * len(shape))
    return pl.pallas_call(
        _mla_prep_kernel,
        grid=(b, s // tm),
        in_specs=[pl.BlockSpec((1, tm, Q_LORA), lambda bi, i: (bi, i, COL_CQ // Q_LORA)),
                  pl.BlockSpec((1, tm, KV_LORA), lambda bi, i: (bi, i, COL_CKV // KV_LORA)),
                  pl.BlockSpec((1, tm, LANES), lambda bi, i: (bi, i, COL_KR // LANES)),
                  const((1, Q_LORA)), const((1, KV_LORA)),
                  const(wuq_p.shape), const(wukv_p.shape),
                  const((1, MLA_QK_PAD)), const((1, MLA_QK_PAD)),
                  pl.BlockSpec((tm, LANES), lambda bi, i: (i, 0)),
                  pl.BlockSpec((tm, LANES), lambda bi, i: (i, 0)),
                  pl.BlockSpec((tm, LANES), lambda bi, i: (i, 0))],
        out_specs=[pl.BlockSpec((1, MLA_HEADS, tm, MLA_QK_PAD), lambda bi, i: (bi, 0, i, 0)),
                   pl.BlockSpec((1, MLA_HEADS, tm, MLA_QK_PAD), lambda bi, i: (bi, 0, i, 0)),
                   pl.BlockSpec((1, MLA_HEADS, tm, MLA_V), lambda bi, i: (bi, 0, i, 0))],
        out_shape=[jax.ShapeDtypeStruct((b, MLA_HEADS, s, MLA_QK_PAD), jnp.bfloat16),
                   jax.ShapeDtypeStruct((b, MLA_HEADS, s, MLA_QK_PAD), jnp.bfloat16),
                   jax.ShapeDtypeStruct((b, MLA_HEADS, s, MLA_V), jnp.bfloat16)],
        compiler_params=_cparams(("parallel", "parallel")),
        name="mla_prep",
    )(proj3, proj3, proj3, g_cq, g_ckv, wuq_p, wukv_p, gq_pad, gk_pad, rope_c, rope_s1, rope_s2)


def _dsa_prep_kernel(q_ref, k_ref, v_ref, qi_ref, ki_ref, gq_ref, gk_ref,
                     qo_ref, ko_ref, vo_ref, qio_ref, kio_ref):
    gq, gk = gq_ref[...], gk_ref[...]
    scale = DSA_HD ** -0.5
    for h in range(DSA_HEADS):
        sl = slice(h * DSA_HD, (h + 1) * DSA_HD)
        qh = q_ref[0, :, sl]
        qo_ref[0, h] = (qh * lax.rsqrt(jnp.mean(qh * qh, axis=-1, keepdims=True) + EPS) * gq * scale
                        ).astype(jnp.bfloat16)
        kh = k_ref[0, :, sl]
        ko_ref[0, h] = (kh * lax.rsqrt(jnp.mean(kh * kh, axis=-1, keepdims=True) + EPS) * gk
                        ).astype(jnp.bfloat16)
        vo_ref[0, h] = v_ref[0, :, sl].astype(jnp.bfloat16)
    qio_ref[0] = (qi_ref[0] * (IDX_HD ** -0.5)).astype(jnp.bfloat16)
    ki = ki_ref[0]
    kio_ref[0, 0] = ki.astype(jnp.bfloat16)
    kio_ref[0, 1] = pltpu.roll(ki, IDX_HD, 1).astype(jnp.bfloat16)


def _dsa_prep(proj3, gq, gk, tm=256):
    b, s, _ = proj3.shape
    w = DSA_HEADS * DSA_HD
    hd_spec = pl.BlockSpec((1, DSA_HEADS, tm, DSA_HD), lambda bi, i: (bi, 0, i, 0))
    hd_shape = jax.ShapeDtypeStruct((b, DSA_HEADS, s, DSA_HD), jnp.bfloat16)
    return pl.pallas_call(
        _dsa_prep_kernel,
        grid=(b, s // tm),
        in_specs=[pl.BlockSpec((1, tm, w), lambda bi, i: (bi, i, COL_Q // w)),
                  pl.BlockSpec((1, tm, w), lambda bi, i: (bi, i, COL_K // w)),
                  pl.BlockSpec((1, tm, w), lambda bi, i: (bi, i, COL_V // w)),
                  pl.BlockSpec((1, tm, w), lambda bi, i: (bi, i, COL_QI // w)),
                  pl.BlockSpec((1, tm, LANES), lambda bi, i: (bi, i, COL_KI // LANES)),
                  pl.BlockSpec((1, DSA_HD), lambda bi, i: (0, 0)),
                  pl.BlockSpec((1, DSA_HD), lambda bi, i: (0, 0))],
        out_specs=[hd_spec, hd_spec, hd_spec,
                   pl.BlockSpec((1, tm, w), lambda bi, i: (bi, i, 0)),
                   pl.BlockSpec((1, 2, tm, LANES), lambda bi, i: (bi, 0, i, 0))],
        out_shape=[hd_shape, hd_shape, hd_shape,
                   jax.ShapeDtypeStruct((b, s, w), jnp.bfloat16),
                   jax.ShapeDtypeStruct((b, 2, s, LANES), jnp.bfloat16)],
        compiler_params=_cparams(("parallel", "parallel")),
        name="dsa_prep",
    )(proj3, proj3, proj3, proj3, proj3, gq, gk)


def _lane_fold(x, op):
    out = x[:, :LANES]
    for c in range(1, x.shape[1] // LANES):
        out = op(out, x[:, c * LANES:(c + 1) * LANES])
    return out


ATT_SLOTS = 8


def _attn_scratch(nsub, n_tiles):
    t = ATT_T
    return ([pltpu.VMEM((nsub, t, LANES), jnp.float32)] * 3
            + [pltpu.VMEM((ATT_SLOTS, t, t), jnp.float32), pltpu.VMEM((ATT_SLOTS, t, t), jnp.bfloat16),
               pltpu.VMEM((nsub * n_tiles, t, t), jnp.float32)])


def _attn_sweep(q_ref, k_ref, v_ref, o_ref, m_scr, l_scr, acc_scr, s_scr, p_scr, all_scr, add_fn):
    qi = pl.program_id(2)
    t = ATT_T
    nsub = q_ref.shape[2] // t
    nrep = t // LANES
    nslot = s_scr.shape[0]
    first = nsub * qi

    def qk(slot, unit):
        r, ki, _ = unit
        off = pl.multiple_of(ki * t, t)
        s_scr[slot] = lax.dot_general(q_ref[0, 0, r * t:(r + 1) * t, :], k_ref[0, 0, pl.ds(off, t), :],
                                      _NT, preferred_element_type=jnp.float32)

    n_tiles = all_scr.shape[0] // nsub

    def fold_max(slot, unit):
        r, ki, _ = unit
        a = add_fn(*unit)
        s = s_scr[slot] if a is None else s_scr[slot] + a
        all_scr[r * n_tiles + ki] = s
        m_scr[r] = jnp.maximum(m_scr[r], _lane_fold(s, jnp.maximum))

    def probs(slot, unit):
        r, ki, _ = unit
        p = jnp.exp(all_scr[r * n_tiles + ki] - _rep(m_scr[r], nrep))
        l_scr[r] += _lane_fold(p, jnp.add)
        p_scr[slot] = p.astype(jnp.bfloat16)

    def pv(slot, unit):
        r, ki, _ = unit
        off = pl.multiple_of(ki * t, t)
        acc_scr[r] += jnp.dot(p_scr[slot], v_ref[0, 0, pl.ds(off, t), :], preferred_element_type=jnp.float32)

    def run(units, stages):
        n = len(units)
        for step in range(n + len(stages) - 1):
            for k, stage in enumerate(stages):
                i = step - k
                if 0 <= i < n:
                    stage(i % nslot, units[i])

    def sweep(stages):
        def body(kp, carry):
            run([(r, 2 * kp + kk, None) for kk in range(2) for r in range(nsub)], stages)
            return carry

        if nsub % 2 == 0:
            lax.fori_loop(0, first // 2, body, 0)
        else:
            lax.fori_loop(0, first, lambda ki, c: (run([(0, ki, None)], stages), c)[1], 0)
        run([(r, first + j, j - r) for j in range(nsub) for r in range(j, nsub)], stages)

    m_scr[...] = jnp.full(m_scr.shape, NEG, jnp.float32)
    sweep((qk, fold_max))
    for r in range(nsub):
        m_scr[r] = jnp.broadcast_to(jnp.max(m_scr[r], axis=1, keepdims=True), (t, LANES))
    l_scr[...] = jnp.zeros(l_scr.shape, jnp.float32)
    acc_scr[...] = jnp.zeros(acc_scr.shape, jnp.float32)
    sweep((probs, pv))
    for r in range(nsub):
        l = jnp.sum(l_scr[r], axis=1, keepdims=True)
        o_ref[0, r * t:(r + 1) * t, :] = (acc_scr[r] / l).astype(o_ref.dtype)


def _attn_nsub(n_tiles):
    return 4 if n_tiles % 4 == 0 else (2 if n_tiles % 2 == 0 else 1)


def _mla_attn_kernel(q_ref, k_ref, v_ref, mask_ref, o_ref, *scr):
    def add_fn(r, ki, rel):
        return mask_ref[...] if rel == 0 else None

    _attn_sweep(q_ref, k_ref, v_ref, o_ref, *scr, add_fn)


def _mla_attn(q, k, v, mask_tbl):
    b, h, s, dq = q.shape
    t = ATT_T
    nsub = _attn_nsub(s // t)
    tq = nsub * t
    return pl.pallas_call(
        _mla_attn_kernel,
        grid=(b, h, s // tq),
        in_specs=[pl.BlockSpec((1, 1, tq, dq), lambda bi, hi, i: (bi, hi, i, 0)),
                  pl.BlockSpec((1, 1, s, dq), lambda bi, hi, i: (bi, hi, 0, 0)),
                  pl.BlockSpec((1, 1, s, MLA_V), lambda bi, hi, i: (bi, hi, 0, 0)),
                  pl.BlockSpec((t, t), lambda bi, hi, i: (0, 0))],
        out_specs=pl.BlockSpec((1, tq, MLA_V), lambda bi, hi, i: (bi, i, hi)),
        out_shape=jax.ShapeDtypeStruct((b, s, h * MLA_V), jnp.bfloat16),
        scratch_shapes=_attn_scratch(nsub, s // t),
        compiler_params=_cparams(("parallel", "parallel", "arbitrary")),
        name="mla_attn",
    )(q, k, v, mask_tbl)


def _dsa_attn_kernel(q_ref, k_ref, v_ref, sel_ref, bias_ref, o_ref, *scr):
    nsub = q_ref.shape[2] // ATT_T
    first = nsub * pl.program_id(2)

    def add_fn(r, ki, rel):
        back = jnp.minimum(first + r - ki, 2) if rel is None else min(-rel, 2)
        return bias_ref[0, back] + sel_ref[0, r, ki].astype(jnp.float32)

    _attn_sweep(q_ref, k_ref, v_ref, o_ref, *scr, add_fn)


def _dsa_attn(q, k, v, sel, bias_tbl):
    b, h, s, d = q.shape
    t = ATT_T
    n = s // t
    nsub = _attn_nsub(n)
    tq = nsub * t
    return pl.pallas_call(
        _dsa_attn_kernel,
        grid=(b, h, s // tq),
        in_specs=[pl.BlockSpec((1, 1, tq, d), lambda bi, hi, i: (bi, hi, i, 0)),
                  pl.BlockSpec((1, 1, s, d), lambda bi, hi, i: (bi, hi, 0, 0)),
                  pl.BlockSpec((1, 1, s, d), lambda bi, hi, i: (bi, hi, 0, 0)),
                  pl.BlockSpec((1, nsub, n, t, t), lambda bi, hi, i: (bi, i, 0, 0, 0)),
                  pl.BlockSpec((1, 3, t, t), lambda bi, hi, i: (hi, 0, 0, 0))],
        out_specs=pl.BlockSpec((1, tq, d), lambda bi, hi, i: (bi, i, hi)),
        out_shape=jax.ShapeDtypeStruct((b, s, h * d), jnp.bfloat16),
        scratch_shapes=_attn_scratch(nsub, s // t),
        compiler_params=_cparams(("parallel", "parallel", "arbitrary")),
        name="dsa_attn",
    )(q, k, v, sel, bias_tbl)


def _bias_kernel(rb_ref, o_ref):
    h = pl.program_id(0)
    t = o_ref.shape[2]
    row = lax.broadcasted_iota(jnp.int32, (t, t), 0)
    col = lax.broadcasted_iota(jnp.int32, (t, t), 1)
    nb = REL_BUCKETS // 2
    max_exact = nb // 2
    for r in range(2):
        rel = col - row - r * t
        n = jnp.abs(rel)
        n2 = n * n
        large = jnp.full((t, t), max_exact, jnp.int32)
        for j in range(1, nb - max_exact):
            large = large + (n2 >= (max_exact * max_exact) * (2 ** j)).astype(jnp.int32)
        bucket = jnp.where(rel > 0, nb, 0) + jnp.where(n < max_exact, n, large)
        val = jnp.zeros((t, t), jnp.float32)
        for bkt in range(REL_BUCKETS):
            val = jnp.where(bucket == bkt, rb_ref[bkt, h], val)
        o_ref[0, r] = val
    o_ref[0, 2] = jnp.full((t, t), rb_ref[nb - 1, h], jnp.float32)


def _bias_tiles(rel_bias):
    t = ATT_T
    assert t >= REL_MAX_DIST
    return pl.pallas_call(
        _bias_kernel,
        grid=(DSA_HEADS,),
        in_specs=[pl.BlockSpec(memory_space=pltpu.SMEM)],
        out_specs=pl.BlockSpec((1, 3, t, t), lambda h: (h, 0, 0, 0)),
        out_shape=jax.ShapeDtypeStruct((DSA_HEADS, 3, t, t), jnp.float32),
        compiler_params=_cparams(("arbitrary",)),
        name="t5_bias",
    )(rel_bias)


_F32_INF_BITS = 0x7F800000
_INT_MIN = np.int32(-2 ** 31)


def _index_kernel(n_sel, q_ref, w_ref, k_ref, o_ref, score_scr, wt_scr):
    qi = pl.program_id(1)
    t = q_ref.shape[1]
    n_tiles = score_scr.shape[0]
    n_adm = qi + 1
    wt_scr[...] = w_ref[0].T
    q_chunk = (qi * t + lax.broadcasted_iota(jnp.int32, (1, t), 1)) // CHUNK

    def score_body(c, carry):
        off = pl.multiple_of(c * t, t)
        acc = jnp.zeros((t, t), jnp.float32)
        for hp in range(IDX_HEADS // 2):
            qpair = q_ref[0, :, hp * LANES:(hp + 1) * LANES]
            for par in range(2):
                h = 2 * hp + par
                kk = k_ref[0, par, pl.ds(off, t), :]
                d = lax.dot_general(kk, qpair, _NT, preferred_element_type=jnp.float32)
                acc = acc + jnp.maximum(d, 0.0) * wt_scr[h:h + 1, :]
        k_chunk = (c * t + lax.broadcasted_iota(jnp.int32, (t, 1), 0)) // CHUNK
        score_scr[c] = jnp.where(k_chunk <= q_chunk, acc, -jnp.inf)
        return carry

    lax.fori_loop(0, n_adm, score_body, 0)

    def count_ge(cand):
        def cnt_body(c, cnt):
            ge = jnp.where(score_scr[c] >= cand, 1.0, 0.0)
            return cnt + jnp.sum(ge.reshape(4, t // 32, 8, t), axis=1)

        cnt = lax.fori_loop(0, n_adm, cnt_body, jnp.zeros((4, 8, t), jnp.float32))
        return jnp.sum(jnp.sum(cnt, axis=0), axis=0, keepdims=True)

    kf = float(n_sel)
    pos = count_ge(jnp.zeros((1, t), jnp.float32)) >= kf

    def thr_of(u):
        neg_bits = jnp.bitwise_or(_F32_INF_BITS - u, _INT_MIN)
        return pltpu.bitcast(jnp.where(pos, u, neg_bits), jnp.float32)

    def bit_body(it, u):
        cand = u + jnp.left_shift(jnp.int32(1), 30 - it)
        ok = jnp.logical_and(cand <= _F32_INF_BITS, count_ge(thr_of(cand)) >= kf)
        return jnp.where(ok, cand, u)

    thr = thr_of(lax.fori_loop(0, 31, bit_body, jnp.zeros((1, t), jnp.int32)))

    def out_body(c, carry):
        score = score_scr[c]
        sel = jnp.logical_and(score >= thr, score > -jnp.inf)
        o_ref[0, 0, c] = jnp.where(sel, 0.0, NEG).T.astype(o_ref.dtype)
        return carry

    lax.fori_loop(0, n_adm, out_body, 0)

    def fill_body(c, carry):
        o_ref[0, 0, c] = jnp.full((t, t), NEG, o_ref.dtype)
        return carry

    lax.fori_loop(n_adm, n_tiles, fill_body, 0)


def _dsa_index(q_idx, w3, k_idx2, n_sel, w_col_block):
    b, s, _ = q_idx.shape
    t = ATT_T
    n = s // t
    return pl.pallas_call(
        functools.partial(_index_kernel, n_sel),
        grid=(b, n),
        in_specs=[pl.BlockSpec((1, t, IDX_HEADS * IDX_HD), lambda bi, i: (bi, i, 0)),
                  pl.BlockSpec((1, t, LANES), lambda bi, i: (bi, i, w_col_block)),
                  pl.BlockSpec((1, 2, s, LANES), lambda bi, i: (bi, 0, 0, 0))],
        out_specs=pl.BlockSpec((1, 1, n, t, t), lambda bi, i: (bi, i, 0, 0, 0)),
        out_shape=jax.ShapeDtypeStruct((b, n, n, t, t), jnp.bfloat16),
        scratch_shapes=[pltpu.VMEM((n, t, t), jnp.float32),
                        pltpu.VMEM((LANES, t), jnp.float32)],
        compiler_params=_cparams(("parallel", "parallel")),
        name="dsa_index",
    )(q_idx, w3, k_idx2)


def _outproj_kernel(x_ref, ya_ref, yb_ref, wa_ref, wb_ref, g_ref, x1_ref, h2t_ref):
    y = (jnp.dot(ya_ref[...], wa_ref[...], preferred_element_type=jnp.float32)
         + jnp.dot(yb_ref[...], wb_ref[...], preferred_element_type=jnp.float32))
    x1 = x_ref[...] + y
    x1_ref[...] = x1
    ms = jnp.mean(x1 * x1, axis=-1, keepdims=True)
    h2t_ref[...] = (x1 * lax.rsqrt(ms + EPS) * g_ref[...]).T.astype(jnp.bfloat16)


def _outproj(x2d, ya, yb, wo_a, wo_b, g_ffn, tm=256):
    m, d = x2d.shape
    ka, kb = ya.shape[1], yb.shape[1]
    return pl.pallas_call(
        _outproj_kernel,
        grid=(m // tm,),
        in_specs=[pl.BlockSpec((tm, d), lambda i: (i, 0)),
                  pl.BlockSpec((tm, ka), lambda i: (i, 0)),
                  pl.BlockSpec((tm, kb), lambda i: (i, 0)),
                  pl.BlockSpec((ka, d), lambda i: (0, 0)),
                  pl.BlockSpec((kb, d), lambda i: (0, 0)),
                  pl.BlockSpec((1, d), lambda i: (0, 0))],
        out_specs=[pl.BlockSpec((tm, d), lambda i: (i, 0)),
                   pl.BlockSpec((d, tm), lambda i: (0, i))],
        out_shape=[jax.ShapeDtypeStruct((m, d), jnp.float32),
                   jax.ShapeDtypeStruct((d, m), jnp.bfloat16)],
        compiler_params=_cparams(("parallel",)),
        name="outproj",
    )(x2d, ya, yb, wo_a, wo_b, g_ffn)


def _sort_network(n):
    def merge(lo, hi, r):
        step = r * 2
        if step < hi - lo:
            yield from merge(lo, hi, step)
            yield from merge(lo + r, hi, step)
            yield from [(i, i + r) for i in range(lo + r, hi - r, step)]
        else:
            yield (lo, lo + r)

    def sort(lo, hi):
        if hi - lo >= 1:
            mid = lo + (hi - lo) // 2
            yield from sort(lo, mid)
            yield from sort(mid + 1, hi)
            yield from merge(lo, hi, 1)

    return list(sort(0, n - 1))


SUBLANES = 8


def _top_values(x, k):
    ngroups = x.shape[0] // SUBLANES
    rows = [x[SUBLANES * i:SUBLANES * (i + 1)] for i in range(ngroups)]
    for i, j in _sort_network(ngroups):
        rows[i], rows[j] = jnp.maximum(rows[i], rows[j]), jnp.minimum(rows[i], rows[j])
    out = []
    for r in range(k):
        m = jnp.max(rows[0], axis=0, keepdims=True)
        out.append(m)
        live = min(ngroups, k - 1 - r)
        pop = rows[0] == m
        for lvl in range(live):
            below = rows[lvl + 1] if lvl + 1 < ngroups else jnp.full_like(rows[lvl], -jnp.inf)
            rows[lvl] = jnp.where(pop, below, rows[lvl])
    return out


_N_TOP = PEER_TOPK + 1
_CAND_PAIRS = [(a, b) for a in range(_N_TOP) for b in range(_N_TOP) if (a + 1) * (b + 1) <= _N_TOP]


def _peer_score_kernel(h_ref, wq_ref, sk_ref, b_ref, beta_ref, a_ref):
    tm = h_ref.shape[1]
    qr_t = jnp.dot(wq_ref[...], h_ref[...], preferred_element_type=jnp.float32)
    for h in range(PEER_HEADS):
        q0 = qr_t[(2 * h) * PEER_NKEYS:(2 * h + 1) * PEER_NKEYS].astype(jnp.bfloat16)
        q1 = qr_t[(2 * h + 1) * PEER_NKEYS:(2 * h + 2) * PEER_NKEYS].astype(jnp.bfloat16)
        s0 = jnp.dot(sk_ref[2 * h], q0, preferred_element_type=jnp.float32)
        s1 = jnp.dot(sk_ref[2 * h + 1], q1, preferred_element_type=jnp.float32)
        top0 = _top_values(s0, _N_TOP)
        top1 = _top_values(s1, _N_TOP)
        cands = [top0[a] + top1[b] for (a, b) in _CAND_PAIRS]
        pad = (-len(cands)) % (8 * SUBLANES)
        cands += [jnp.full((1, tm), -jnp.inf, jnp.float32)] * pad
        vals = _top_values(jnp.concatenate(cands, axis=0), _N_TOP)
        z = jnp.zeros((1, tm), jnp.float32)
        for v in vals[:PEER_TOPK]:
            z = z + jnp.exp(v - vals[0])
        cut = 0.5 * (vals[PEER_TOPK - 1] + vals[PEER_TOPK])
        b_ref[h] = jnp.exp(s1 - top1[0]).astype(b_ref.dtype)
        beta_ref[h] = jnp.exp((cut - top1[0]) - s0)
        a_ref[h] = jnp.exp(s0 - top0[0]) / z


def _peer_scores(h2t, wpq_t, sk, tm=256):
    d, m = h2t.shape
    nq = wpq_t.shape[0]
    out_spec = pl.BlockSpec((PEER_HEADS, PEER_NKEYS, tm), lambda i: (0, 0, i))
    out_shape = jax.ShapeDtypeStruct((PEER_HEADS, PEER_NKEYS, m), jnp.float32)
    return pl.pallas_call(
        _peer_score_kernel,
        grid=(m // tm,),
        in_specs=[pl.BlockSpec((d, tm), lambda i: (0, i)),
                  pl.BlockSpec((nq, d), lambda i: (0, 0)),
                  pl.BlockSpec(sk.shape, lambda i: (0, 0, 0))],
        out_specs=[out_spec, out_spec, out_spec],
        out_shape=[jax.ShapeDtypeStruct(out_shape.shape, jnp.bfloat16), out_shape, out_shape],
        compiler_params=_cparams(("parallel",)),
        name="peer_scores",
    )(h2t, wpq_t, sk)


def _gelu_exact(x):
    return 0.5 * x * (1.0 + lax.erf(x * (1.0 / math.sqrt(2.0))))


PEER_KC = 256


def _peer_main_step(h_ref, u_ref, vt_ref, b_ref, beta_ref, a_ref, acc_scr, w_scr, act_next, act_cur):
    tm = h_ref.shape[1]
    te = u_ref.shape[0]
    act_next[...] = jnp.dot(u_ref[...], h_ref[...], preferred_element_type=jnp.float32)
    parts = []
    for c in range(te // PEER_KC):
        for il in range(c * PEER_KC // PEER_NKEYS, (c + 1) * PEER_KC // PEER_NKEYS):
            rows = slice(il * PEER_NKEYS, (il + 1) * PEER_NKEYS)
            for lt in range(tm // LANES):
                lanes = slice(lt * LANES, (lt + 1) * LANES)
                g = None
                for h in range(PEER_HEADS):
                    b = b_ref[h, :, lanes]
                    beta = beta_ref[h, il:il + 1, lanes].astype(jnp.bfloat16)
                    a = a_ref[h, il:il + 1, lanes].astype(jnp.bfloat16)
                    term = a * jnp.where(b >= beta, b, jnp.zeros_like(b))
                    g = term if g is None else g + term
                w_scr[rows, lanes] = g * _gelu_exact(act_cur[rows, lanes]).astype(jnp.bfloat16)
        ck = slice(c * PEER_KC, (c + 1) * PEER_KC)
        parts.append(jnp.dot(vt_ref[:, ck], w_scr[ck, :], preferred_element_type=jnp.float32))
    acc_scr[...] += functools.reduce(lambda p, q: p + q, parts)


def _peer_main_kernel(h_ref, u_ref, vt_ref, b_ref, beta_ref, a_ref, x1_ref, o_ref,
                      acc_scr, w_scr, act_a, act_b):
    e = pl.program_id(1)

    @pl.when(e == 0)
    def _():
        acc_scr[...] = jnp.zeros(acc_scr.shape, jnp.float32)
        act_b[...] = jnp.zeros(act_b.shape, jnp.float32)

    args = (h_ref, u_ref, vt_ref, b_ref, beta_ref, a_ref, acc_scr, w_scr)

    @pl.when(e % 2 == 0)
    def _():
        _peer_main_step(*args, act_a, act_b)

    @pl.when(e % 2 == 1)
    def _():
        _peer_main_step(*args, act_b, act_a)

    @pl.when(e == pl.num_programs(1) - 1)
    def _():
        o_ref[...] = x1_ref[...] + acc_scr[...].T


def _peer_main(h2t, u_bf, vt_bf, bs, betas, a_s, x1, tm=512, te=1024):
    d, m = h2t.shape
    ne = u_bf.shape[0] // te
    ni = te // PEER_NKEYS
    back = lambda e, n: jnp.clip(e - n, 0, ne - 1)
    once = pl.Buffered(1)
    return pl.pallas_call(
        _peer_main_kernel,
        grid=(m // tm, ne + 1),
        in_specs=[pl.BlockSpec((d, tm), lambda i, e: (0, i), pipeline_mode=once),
                  pl.BlockSpec((te, d), lambda i, e: (back(e, 0), 0)),
                  pl.BlockSpec((d, te), lambda i, e: (0, back(e, 1))),
                  pl.BlockSpec((PEER_HEADS, PEER_NKEYS, tm), lambda i, e: (0, 0, i), pipeline_mode=once),
                  pl.BlockSpec((PEER_HEADS, ni, tm), lambda i, e: (0, back(e, 1), i)),
                  pl.BlockSpec((PEER_HEADS, ni, tm), lambda i, e: (0, back(e, 1), i)),
                  pl.BlockSpec((tm, d), lambda i, e: (i, 0), pipeline_mode=once)],
        out_specs=pl.BlockSpec((tm, d), lambda i, e: (i, 0)),
        out_shape=jax.ShapeDtypeStruct((m, d), jnp.float32),
        scratch_shapes=[pltpu.VMEM((d, tm), jnp.float32),
                        pltpu.VMEM((te, tm), jnp.bfloat16),
                        pltpu.VMEM((te, tm), jnp.float32),
                        pltpu.VMEM((te, tm), jnp.float32)],
        compiler_params=_cparams(("parallel", "arbitrary")),
        name="peer_main",
    )(h2t, u_bf, vt_bf, bs, betas, a_s, x1)


def _rope_tables(s):
    half = MLA_ROPE // 2
    freqs = ROPE_THETA ** (-2.0 * jnp.arange(half, dtype=jnp.float32) / MLA_ROPE)
    ang = jnp.arange(s, dtype=jnp.float32)[:, None] * freqs[None, :]
    cos, sin = jnp.cos(ang), jnp.sin(ang)
    z = jnp.zeros_like(cos)
    c = jnp.concatenate([cos, cos, z, z], axis=1)
    s1 = jnp.concatenate([z, sin, z, z], axis=1)
    s2 = jnp.concatenate([-sin, z, z, z], axis=1)
    return c, s1, s2


def _mla_mask_table():
    t = ATT_T
    r = np.arange(t)[:, None] // CHUNK
    c = np.arange(t)[None, :] // CHUNK
    return jnp.asarray(np.where(c <= r, 0.0, NEG).astype(np.float32))


def _layer(x, g_attn, w_in, g_cq, g_ckv, w_uq, w_ukv, g_mla_q, g_mla_k, g_dsa_q, g_dsa_k,
           rel_bias, w_o, g_ffn, w_pq, sub_keys, u_tab, v_tab):
    b, s, d = x.shape
    m = b * s
    bf = jnp.bfloat16
    f32 = jnp.float32
    assert s % ATT_T == 0 and m % 512 == 0

    o = np.cumsum([0, Q_LORA, KV_LORA, MLA_ROPE, 1024, 1024, 1024, 1024, IDX_HD, IDX_HEADS])
    zc = lambda n: jnp.zeros((d, n), bf)
    w_in_b = w_in.astype(bf)
    w_in_p = jnp.concatenate([
        w_in_b[:, o[3]:o[7]], w_in_b[:, o[0]:o[2]],
        w_in_b[:, o[2]:o[3]], zc(LANES - MLA_ROPE),
        w_in_b[:, o[7]:o[8]], zc(LANES - IDX_HD),
        w_in_b[:, o[8]:o[9]], zc(LANES - IDX_HEADS), zc(IN_PAD - COL_WI - LANES)], axis=1)
    wuq_p = jnp.pad(w_uq.reshape(Q_LORA, MLA_HEADS, MLA_QK),
                    ((0, 0), (0, 0), (0, MLA_QK_PAD - MLA_QK))).reshape(Q_LORA, -1).astype(bf)
    wukv_p = w_ukv.reshape(KV_LORA, MLA_HEADS, 2, MLA_NOPE).transpose(0, 2, 1, 3).reshape(KV_LORA, -1).astype(bf)
    gq_pad = jnp.pad(g_mla_q, (0, MLA_QK_PAD - MLA_QK))[None]
    gk_pad = jnp.pad(g_mla_k, (0, MLA_QK_PAD - MLA_QK))[None]
    rope_c, rope_s1, rope_s2 = _rope_tables(s)

    proj = _inproj(x.reshape(m, d), g_attn[None], w_in_p)
    proj3 = proj.reshape(b, s, IN_PAD)

    q_a, k_a, v_a = _mla_prep(proj3, g_cq[None], g_ckv[None], wuq_p, wukv_p, gq_pad, gk_pad,
                              rope_c, rope_s1, rope_s2)
    y_a = _mla_attn(q_a, k_a, v_a, _mla_mask_table())

    q_d, k_d, v_d, q_i, k_i2 = _dsa_prep(proj3, g_dsa_q[None], g_dsa_k[None])
    n_sel = min(DSA_TOPK, s // 4)
    sel = _dsa_index(q_i, proj3, k_i2, n_sel, COL_WI // LANES)
    y_b = _dsa_attn(q_d, k_d, v_d, sel, _bias_tiles(rel_bias))

    na = MLA_HEADS * MLA_V
    x1, h2t = _outproj(x.reshape(m, d), y_a.reshape(m, na), y_b.reshape(m, -1),
                      w_o[:na].astype(bf), w_o[na:].astype(bf), g_ffn[None])

    sk = sub_keys.reshape(PEER_HEADS * 2, PEER_NKEYS, -1).astype(bf)
    bs, betas, a_s = _peer_scores(h2t, w_pq.T.astype(bf), sk)
    x2 = _peer_main(h2t, u_tab.astype(bf), v_tab.T.astype(bf), bs, betas, a_s, x1)
    return x2.reshape(b, s, d)


def kernel(x, g_attn, w_in, g_cq, g_ckv, w_uq, w_ukv, g_mla_q, g_mla_k, g_dsa_q, g_dsa_k,
           rel_bias, w_o, g_ffn, w_pq, sub_keys, u_tab, v_tab):
    for l in range(w_in.shape[0]):
        x = _layer(x, g_attn[l], w_in[l], g_cq[l], g_ckv[l], w_uq[l], w_ukv[l], g_mla_q[l], g_mla_k[l],
                   g_dsa_q[l], g_dsa_k[l], rel_bias, w_o[l], g_ffn[l], w_pq[l], sub_keys[l],
                   u_tab[l], v_tab[l])
    return x
```

```python
import functools
import math

import numpy as np
import jax
import jax.numpy as jnp
from jax import lax
from jax.experimental import pallas as pl
from jax.experimental.pallas import tpu as pltpu

EPS = 1e-6
CHUNK = 64

MLA_HEADS = 8
MLA_NOPE = 128
MLA_ROPE = 64
MLA_V = 128
MLA_QK = MLA_NOPE + MLA_ROPE
MLA_QK_PAD = 256
Q_LORA = 512
KV_LORA = 256
ROPE_THETA = 10000.0

DSA_HEADS = 8
DSA_HD = 128
IDX_HEADS = 16
IDX_HD = 64
DSA_TOPK = 256

REL_BUCKETS = 32
REL_MAX_DIST = 128

PEER_HEADS = 8
PEER_NKEYS = 128
PEER_TOPK = 16

LANES = 128
NEG = -1e30
VMEM_LIMIT = 56 * 1024 * 1024

COL_Q, COL_K, COL_V, COL_QI = 0, 1024, 2048, 3072
COL_CQ, COL_CKV, COL_KR, COL_KI, COL_WI = 4096, 4608, 4864, 4992, 5120
IN_PAD = 5376

ATT_T = 256

_NT = (((1,), (1,)), ((), ()))


def _cparams(sem):
    return pltpu.CompilerParams(dimension_semantics=sem, vmem_limit_bytes=VMEM_LIMIT)


def _rep(x, n):
    return x if n == 1 else jnp.concatenate([x] * n, axis=1)


def _inproj_kernel(x_ref, g_ref, w_ref, o_ref, h_scr):
    @pl.when(pl.program_id(1) == 0)
    def _():
        x = x_ref[...]
        ms = jnp.mean(x * x, axis=-1, keepdims=True)
        h_scr[...] = (x * lax.rsqrt(ms + EPS) * g_ref[...]).astype(jnp.bfloat16)

    o_ref[...] = jnp.dot(h_scr[...], w_ref[...], preferred_element_type=jnp.float32)


def _inproj(x2d, g, w_p, tm=1024, tn=768):
    m, d = x2d.shape
    tm = min(tm, m)
    n = w_p.shape[1]
    return pl.pallas_call(
        _inproj_kernel,
        grid=(m // tm, n // tn),
        in_specs=[pl.BlockSpec((tm, d), lambda i, j: (i, 0)),
                  pl.BlockSpec((1, d), lambda i, j: (0, 0)),
                  pl.BlockSpec((d, tn), lambda i, j: (0, j))],
        out_specs=pl.BlockSpec((tm, tn), lambda i, j: (i, j)),
        out_shape=jax.ShapeDtypeStruct((m, n), jnp.float32),
        scratch_shapes=[pltpu.VMEM((tm, d), jnp.bfloat16)],
        compiler_params=_cparams(("parallel", "arbitrary")),
        name="inproj",
    )(x2d, g, w_p)


def _rope128(xr, c, s1, s2):
    return xr * c + pltpu.roll(xr, 96, 1) * s2 + pltpu.roll(xr, 32, 1) * s1


def _mla_prep_kernel(cq_ref, ckv_ref, kr_ref, gcq_ref, gckv_ref, wuq_ref, wukv_ref, gq_ref, gk_ref,
                     c_ref, s1_ref, s2_ref, q_ref, k_ref, v_ref):
    cq = cq_ref[0]
    cqn = (cq * lax.rsqrt(jnp.mean(cq * cq, axis=-1, keepdims=True) + EPS) * gcq_ref[...])
    q_all = jnp.dot(cqn.astype(jnp.bfloat16), wuq_ref[...], preferred_element_type=jnp.float32)
    ckv = ckv_ref[0]
    ckvn = (ckv * lax.rsqrt(jnp.mean(ckv * ckv, axis=-1, keepdims=True) + EPS) * gckv_ref[...])
    kv = jnp.dot(ckvn.astype(jnp.bfloat16), wukv_ref[...], preferred_element_type=jnp.float32)
    kr = kr_ref[0]
    kr_ss = jnp.sum(kr * kr, axis=-1, keepdims=True)
    c, s1, s2 = c_ref[...], s1_ref[...], s2_ref[...]
    gq, gk = gq_ref[...], gk_ref[...]
    scale = MLA_QK ** -0.5
    for h in range(MLA_HEADS):
        qh = q_all[:, h * MLA_QK_PAD:(h + 1) * MLA_QK_PAD]
        rq = lax.rsqrt(jnp.sum(qh * qh, axis=-1, keepdims=True) * (1.0 / MLA_QK) + EPS)
        qn = qh * rq * gq
        q_ref[0, h, :, :MLA_NOPE] = (qn[:, :MLA_NOPE] * scale).astype(jnp.bfloat16)
        q_ref[0, h, :, MLA_NOPE:] = (_rope128(qn[:, MLA_NOPE:], c, s1, s2) * scale).astype(jnp.bfloat16)
        kn = kv[:, h * MLA_NOPE:(h + 1) * MLA_NOPE]
        rk = lax.rsqrt((jnp.sum(kn * kn, axis=-1, keepdims=True) + kr_ss) * (1.0 / MLA_QK) + EPS)
        k_ref[0, h, :, :MLA_NOPE] = (kn * rk * gk[:, :MLA_NOPE]).astype(jnp.bfloat16)
        k_ref[0, h, :, MLA_NOPE:] = _rope128(kr * rk * gk[:, MLA_NOPE:], c, s1, s2).astype(jnp.bfloat16)
        v_ref[0, h] = kv[:, MLA_HEADS * MLA_NOPE + h * MLA_V:MLA_HEADS * MLA_NOPE + (h + 1) * MLA_V
                         ].astype(jnp.bfloat16)


def _mla_prep(proj3, g_cq, g_ckv, wuq_p, wukv_p, gq_pad, gk_pad, rope_c, rope_s1, rope_s2, tm=256):
    b, s, _ = proj3.shape
    const = lambda shape: pl.BlockSpec(shape, lambda bi, i: (0,) * len(shape))
    return pl.pallas_call(
        _mla_prep_kernel,
        grid=(b, s // tm),
        in_specs=[pl.BlockSpec((1, tm, Q_LORA), lambda bi, i: (bi, i, COL_CQ // Q_LORA)),
                  pl.BlockSpec((1, tm, KV_LORA), lambda bi, i: (bi, i, COL_CKV // KV_LORA)),
                  pl.BlockSpec((1, tm, LANES), lambda bi, i: (bi, i, COL_KR // LANES)),
                  const((1, Q_LORA)), const((1, KV_LORA)),
                  const(wuq_p.shape), const(wukv_p.shape),
                  const((1, MLA_QK_PAD)), const((1, MLA_QK_PAD)),
                  pl.BlockSpec((tm, LANES), lambda bi, i: (i, 0)),
                  pl.BlockSpec((tm, LANES), lambda bi, i: (i, 0)),
                  pl.BlockSpec((tm, LANES), lambda bi, i: (i, 0))],
        out_specs=[pl.BlockSpec((1, MLA_HEADS, tm, MLA_QK_PAD), lambda bi, i: (bi, 0, i, 0)),
                   pl.BlockSpec((1, MLA_HEADS, tm, MLA_QK_PAD), lambda bi, i: (bi, 0, i, 0)),
                   pl.BlockSpec((1, MLA_HEADS, tm, MLA_V), lambda bi, i: (bi, 0, i, 0))],
        out_shape=[jax.ShapeDtypeStruct((b, MLA_HEADS, s, MLA_QK_PAD), jnp.bfloat16),
                   jax.ShapeDtypeStruct((b, MLA_HEADS, s, MLA_QK_PAD), jnp.bfloat16),
                   jax.ShapeDtypeStruct((b, MLA_HEADS, s, MLA_V), jnp.bfloat16)],
        compiler_params=_cparams(("parallel", "parallel")),
        name="mla_prep",
    )(proj3, proj3, proj3, g_cq, g_ckv, wuq_p, wukv_p, gq_pad, gk_pad, rope_c, rope_s1, rope_s2)


def _dsa_prep_kernel(q_ref, k_ref, v_ref, qi_ref, ki_ref, gq_ref, gk_ref,
                     qo_ref, ko_ref, vo_ref, qio_ref, kio_ref):
    gq, gk = gq_ref[...], gk_ref[...]
    scale = DSA_HD ** -0.5
    for h in range(DSA_HEADS):
        sl = slice(h * DSA_HD, (h + 1) * DSA_HD)
        qh = q_ref[0, :, sl]
        qo_ref[0, h] = (qh * lax.rsqrt(jnp.mean(qh * qh, axis=-1, keepdims=True) + EPS) * gq * scale
                        ).astype(jnp.bfloat16)
        kh = k_ref[0, :, sl]
        ko_ref[0, h] = (kh * lax.rsqrt(jnp.mean(kh * kh, axis=-1, keepdims=True) + EPS) * gk
                        ).astype(jnp.bfloat16)
        vo_ref[0, h] = v_ref[0, :, sl].astype(jnp.bfloat16)
    qio_ref[0] = (qi_ref[0] * (IDX_HD ** -0.5)).astype(jnp.bfloat16)
    ki = ki_ref[0]
    kio_ref[0, 0] = ki.astype(jnp.bfloat16)
    kio_ref[0, 1] = pltpu.roll(ki, IDX_HD, 1).astype(jnp.bfloat16)


def _dsa_prep(proj3, gq, gk, tm=256):
    b, s, _ = proj3.shape
    w = DSA_HEADS * DSA_HD
    hd_spec = pl.BlockSpec((1, DSA_HEADS, tm, DSA_HD), lambda bi, i: (bi, 0, i, 0))
    hd_shape = jax.ShapeDtypeStruct((b, DSA_HEADS, s, DSA_HD), jnp.bfloat16)
    return pl.pallas_call(
        _dsa_prep_kernel,
        grid=(b, s // tm),
        in_specs=[pl.BlockSpec((1, tm, w), lambda bi, i: (bi, i, COL_Q // w)),
                  pl.BlockSpec((1, tm, w), lambda bi, i: (bi, i, COL_K // w)),
                  pl.BlockSpec((1, tm, w), lambda bi, i: (bi, i, COL_V // w)),
                  pl.BlockSpec((1, tm, w), lambda bi, i: (bi, i, COL_QI // w)),
                  pl.BlockSpec((1, tm, LANES), lambda bi, i: (bi, i, COL_KI // LANES)),
                  pl.BlockSpec((1, DSA_HD), lambda bi, i: (0, 0)),
                  pl.BlockSpec((1, DSA_HD), lambda bi, i: (0, 0))],
        out_specs=[hd_spec, hd_spec, hd_spec,
                   pl.BlockSpec((1, tm, w), lambda bi, i: (bi, i, 0)),
                   pl.BlockSpec((1, 2, tm, LANES), lambda bi, i: (bi, 0, i, 0))],
        out_shape=[hd_shape, hd_shape, hd_shape,
                   jax.ShapeDtypeStruct((b, s, w), jnp.bfloat16),
                   jax.ShapeDtypeStruct((b, 2, s, LANES), jnp.bfloat16)],
        compiler_params=_cparams(("parallel", "parallel")),
        name="dsa_prep",
    )(proj3, proj3, proj3, proj3, proj3, gq, gk)


def _lane_fold(x, op):
    out = x[:, :LANES]
    for c in range(1, x.shape[1] // LANES):
        out = op(out, x[:, c * LANES:(c + 1) * LANES])
    return out


ATT_SLOTS = 8


def _attn_scratch(nsub, n_tiles):
    t = ATT_T
    return ([pltpu.VMEM((nsub, t, LANES), jnp.float32)] * 3
            + [pltpu.VMEM((ATT_SLOTS, t, t), jnp.float32), pltpu.VMEM((ATT_SLOTS, t, t), jnp.bfloat16),
               pltpu.VMEM((nsub * n_tiles, t, t), jnp.float32)])


def _attn_sweep(q_ref, k_ref, v_ref, o_ref, m_scr, l_scr, acc_scr, s_scr, p_scr, all_scr, add_fn):
    qi = pl.program_id(2)
    t = ATT_T
    nsub = q_ref.shape[2] // t
    nrep = t // LANES
    nslot = s_scr.shape[0]
    first = nsub * qi

    def qk(slot, unit):
        r, ki, _ = unit
        off = pl.multiple_of(ki * t, t)
        s_scr[slot] = lax.dot_general(q_ref[0, 0, r * t:(r + 1) * t, :], k_ref[0, 0, pl.ds(off, t), :],
                                      _NT, preferred_element_type=jnp.float32)

    n_tiles = all_scr.shape[0] // nsub

    def fold_max(slot, unit):
        r, ki, _ = unit
        a = add_fn(*unit)
        s = s_scr[slot] if a is None else s_scr[slot] + a
        all_scr[r * n_tiles + ki] = s
        m_scr[r] = jnp.maximum(m_scr[r], _lane_fold(s, jnp.maximum))

    def probs(slot, unit):
        r, ki, _ = unit
        p = jnp.exp(all_scr[r * n_tiles + ki] - _rep(m_scr[r], nrep))
        l_scr[r] += _lane_fold(p, jnp.add)
        p_scr[slot] = p.astype(jnp.bfloat16)

    def pv(slot, unit):
        r, ki, _ = unit
        off = pl.multiple_of(ki * t, t)
        acc_scr[r] += jnp.dot(p_scr[slot], v_ref[0, 0, pl.ds(off, t), :], preferred_element_type=jnp.float32)

    def run(units, stages):
        n = len(units)
        for step in range(n + len(stages) - 1):
            for k, stage in enumerate(stages):
                i = step - k
                if 0 <= i < n:
                    stage(i % nslot, units[i])

    def sweep(stages):
        def body(kp, carry):
            run([(r, 2 * kp + kk, None) for kk in range(2) for r in range(nsub)], stages)
            return carry

        if nsub % 2 == 0:
            lax.fori_loop(0, first // 2, body, 0)
        else:
            lax.fori_loop(0, first, lambda ki, c: (run([(0, ki, None)], stages), c)[1], 0)
        run([(r, first + j, j - r) for j in range(nsub) for r in range(j, nsub)], stages)

    m_scr[...] = jnp.full(m_scr.shape, NEG, jnp.float32)
    sweep((qk, fold_max))
    for r in range(nsub):
        m_scr[r] = jnp.broadcast_to(jnp.max(m_scr[r], axis=1, keepdims=True), (t, LANES))
    l_scr[...] = jnp.zeros(l_scr.shape, jnp.float32)
    acc_scr[...] = jnp.zeros(acc_scr.shape, jnp.float32)
    sweep((probs, pv))
    for r in range(nsub):
        l = jnp.sum(l_scr[r], axis=1, keepdims=True)
        o_ref[0, r * t:(r + 1) * t, :] = (acc_scr[r] / l).astype(o_ref.dtype)


def _attn_nsub(n_tiles):
    return 4 if n_tiles % 4 == 0 else (2 if n_tiles % 2 == 0 else 1)


def _mla_attn_kernel(q_ref, k_ref, v_ref, mask_ref, o_ref, *scr):
    def add_fn(r, ki, rel):
        return mask_ref[...] if rel == 0 else None

    _attn_sweep(q_ref, k_ref, v_ref, o_ref, *scr, add_fn)


def _mla_attn(q, k, v, mask_tbl):
    b, h, s, dq = q.shape
    t = ATT_T
    nsub = _attn_nsub(s // t)
    tq = nsub * t
    return pl.pallas_call(
        _mla_attn_kernel,
        grid=(b, h, s // tq),
        in_specs=[pl.BlockSpec((1, 1, tq, dq), lambda bi, hi, i: (bi, hi, i, 0)),
                  pl.BlockSpec((1, 1, s, dq), lambda bi, hi, i: (bi, hi, 0, 0)),
                  pl.BlockSpec((1, 1, s, MLA_V), lambda bi, hi, i: (bi, hi, 0, 0)),
                  pl.BlockSpec((t, t), lambda bi, hi, i: (0, 0))],
        out_specs=pl.BlockSpec((1, tq, MLA_V), lambda bi, hi, i: (bi, i, hi)),
        out_shape=jax.ShapeDtypeStruct((b, s, h * MLA_V), jnp.bfloat16),
        scratch_shapes=_attn_scratch(nsub, s // t),
        compiler_params=_cparams(("parallel", "parallel", "arbitrary")),
        name="mla_attn",
    )(q, k, v, mask_tbl)


def _dsa_attn_kernel(q_ref, k_ref, v_ref, sel_ref, bias_ref, o_ref, *scr):
    nsub = q_ref.shape[2] // ATT_T
    first = nsub * pl.program_id(2)

    def add_fn(r, ki, rel):
        back = jnp.minimum(first + r - ki, 2) if rel is None else min(-rel, 2)
        return bias_ref[0, back] + sel_ref[0, r, ki].astype(jnp.float32)

    _attn_sweep(q_ref, k_ref, v_ref, o_ref, *scr, add_fn)


def _dsa_attn(q, k, v, sel, bias_tbl):
    b, h, s, d = q.shape
    t = ATT_T
    n = s // t
    nsub = _attn_nsub(n)
    tq = nsub * t
    return pl.pallas_call(
        _dsa_attn_kernel,
        grid=(b, h, s // tq),
        in_specs=[pl.BlockSpec((1, 1, tq, d), lambda bi, hi, i: (bi, hi, i, 0)),
                  pl.BlockSpec((1, 1, s, d), lambda bi, hi, i: (bi, hi, 0, 0)),
                  pl.BlockSpec((1, 1, s, d), lambda bi, hi, i: (bi, hi, 0, 0)),
                  pl.BlockSpec((1, nsub, n, t, t), lambda bi, hi, i: (bi, i, 0, 0, 0)),
                  pl.BlockSpec((1, 3, t, t), lambda bi, hi, i: (hi, 0, 0, 0))],
        out_specs=pl.BlockSpec((1, tq, d), lambda bi, hi, i: (bi, i, hi)),
        out_shape=jax.ShapeDtypeStruct((b, s, h * d), jnp.bfloat16),
        scratch_shapes=_attn_scratch(nsub, s // t),
        compiler_params=_cparams(("parallel", "parallel", "arbitrary")),
        name="dsa_attn",
    )(q, k, v, sel, bias_tbl)


def _bias_kernel(rb_ref, o_ref):
    h = pl.program_id(0)
    t = o_ref.shape[2]
    row = lax.broadcasted_iota(jnp.int32, (t, t), 0)
    col = lax.broadcasted_iota(jnp.int32, (t, t), 1)
    nb = REL_BUCKETS // 2
    max_exact = nb // 2
    for r in range(2):
        rel = col - row - r * t
        n = jnp.abs(rel)
        n2 = n * n
        large = jnp.full((t, t), max_exact, jnp.int32)
        for j in range(1, nb - max_exact):
            large = large + (n2 >= (max_exact * max_exact) * (2 ** j)).astype(jnp.int32)
        bucket = jnp.where(rel > 0, nb, 0) + jnp.where(n < max_exact, n, large)
        val = jnp.zeros((t, t), jnp.float32)
        for bkt in range(REL_BUCKETS):
            val = jnp.where(bucket == bkt, rb_ref[bkt, h], val)
        o_ref[0, r] = val
    o_ref[0, 2] = jnp.full((t, t), rb_ref[nb - 1, h], jnp.float32)


def _bias_tiles(rel_bias):
    t = ATT_T
    assert t >= REL_MAX_DIST
    return pl.pallas_call(
        _bias_kernel,
        grid=(DSA_HEADS,),
        in_specs=[pl.BlockSpec(memory_space=pltpu.SMEM)],
        out_specs=pl.BlockSpec((1, 3, t, t), lambda h: (h, 0, 0, 0)),
        out_shape=jax.ShapeDtypeStruct((DSA_HEADS, 3, t, t), jnp.float32),
        compiler_params=_cparams(("arbitrary",)),
        name="t5_bias",
    )(rel_bias)


_F32_INF_BITS = 0x7F800000
_INT_MIN = np.int32(-2 ** 31)


def _index_kernel(n_sel, q_ref, w_ref, k_ref, o_ref, score_scr, wt_scr):
    qi = pl.program_id(1)
    t = q_ref.shape[1]
    n_tiles = score_scr.shape[0]
    n_adm = qi + 1
    wt_scr[...] = w_ref[0].T
    q_chunk = (qi * t + lax.broadcasted_iota(jnp.int32, (1, t), 1)) // CHUNK

    def score_body(c, carry):
        off = pl.multiple_of(c * t, t)
        acc = jnp.zeros((t, t), jnp.float32)
        for hp in range(IDX_HEADS // 2):
            qpair = q_ref[0, :, hp * LANES:(hp + 1) * LANES]
            for par in range(2):
                h = 2 * hp + par
                kk = k_ref[0, par, pl.ds(off, t), :]
                d = lax.dot_general(kk, qpair, _NT, preferred_element_type=jnp.float32)
                acc = acc + jnp.maximum(d, 0.0) * wt_scr[h:h + 1, :]
        k_chunk = (c * t + lax.broadcasted_iota(jnp.int32, (t, 1), 0)) // CHUNK
        score_scr[c] = jnp.where(k_chunk <= q_chunk, acc, -jnp.inf)
        return carry

    lax.fori_loop(0, n_adm, score_body, 0)

    def count_ge(cand):
        def cnt_body(c, cnt):
            ge = jnp.where(score_scr[c] >= cand, 1.0, 0.0)
            return cnt + jnp.sum(ge.reshape(4, t // 32, 8, t), axis=1)

        cnt = lax.fori_loop(0, n_adm, cnt_body, jnp.zeros((4, 8, t), jnp.float32))
        return jnp.sum(jnp.sum(cnt, axis=0), axis=0, keepdims=True)

    kf = float(n_sel)
    pos = count_ge(jnp.zeros((1, t), jnp.float32)) >= kf

    def thr_of(u):
        neg_bits = jnp.bitwise_or(_F32_INF_BITS - u, _INT_MIN)
        return pltpu.bitcast(jnp.where(pos, u, neg_bits), jnp.float32)

    def bit_body(it, u):
        cand = u + jnp.left_shift(jnp.int32(1), 30 - it)
        ok = jnp.logical_and(cand <= _F32_INF_BITS, count_ge(thr_of(cand)) >= kf)
        return jnp.where(ok, cand, u)

    thr = thr_of(lax.fori_loop(0, 31, bit_body, jnp.zeros((1, t), jnp.int32)))

    def out_body(c, carry):
        score = score_scr[c]
        sel = jnp.logical_and(score >= thr, score > -jnp.inf)
        o_ref[0, 0, c] = jnp.where(sel, 0.0, NEG).T.astype(o_ref.dtype)
        return carry

    lax.fori_loop(0, n_adm, out_body, 0)

    def fill_body(c, carry):
        o_ref[0, 0, c] = jnp.full((t, t), NEG, o_ref.dtype)
        return carry

    lax.fori_loop(n_adm, n_tiles, fill_body, 0)


def _dsa_index(q_idx, w3, k_idx2, n_sel, w_col_block):
    b, s, _ = q_idx.shape
    t = ATT_T
    n = s // t
    return pl.pallas_call(
        functools.partial(_index_kernel, n_sel),
        grid=(b, n),
        in_specs=[pl.BlockSpec((1, t, IDX_HEADS * IDX_HD), lambda bi, i: (bi, i, 0)),
                  pl.BlockSpec((1, t, LANES), lambda bi, i: (bi, i, w_col_block)),
                  pl.BlockSpec((1, 2, s, LANES), lambda bi, i: (bi, 0, 0, 0))],
        out_specs=pl.BlockSpec((1, 1, n, t, t), lambda bi, i: (bi, i, 0, 0, 0)),
        out_shape=jax.ShapeDtypeStruct((b, n, n, t, t), jnp.bfloat16),
        scratch_shapes=[pltpu.VMEM((n, t, t), jnp.float32),
                        pltpu.VMEM((LANES, t), jnp.float32)],
        compiler_params=_cparams(("parallel", "parallel")),
        name="dsa_index",
    )(q_idx, w3, k_idx2)


def _outproj_kernel(x_ref, ya_ref, yb_ref, wa_ref, wb_ref, g_ref, x1_ref, h2t_ref):
    y = (jnp.dot(ya_ref[...], wa_ref[...], preferred_element_type=jnp.float32)
         + jnp.dot(yb_ref[...], wb_ref[...], preferred_element_type=jnp.float32))
    x1 = x_ref[...] + y
    x1_ref[...] = x1
    ms = jnp.mean(x1 * x1, axis=-1, keepdims=True)
    h2t_ref[...] = (x1 * lax.rsqrt(ms + EPS) * g_ref[...]).T.astype(jnp.bfloat16)


def _outproj(x2d, ya, yb, wo_a, wo_b, g_ffn, tm=256):
    m, d = x2d.shape
    ka, kb = ya.shape[1], yb.shape[1]
    return pl.pallas_call(
        _outproj_kernel,
        grid=(m // tm,),
        in_specs=[pl.BlockSpec((tm, d), lambda i: (i, 0)),
                  pl.BlockSpec((tm, ka), lambda i: (i, 0)),
                  pl.BlockSpec((tm, kb), lambda i: (i, 0)),
                  pl.BlockSpec((ka, d), lambda i: (0, 0)),
                  pl.BlockSpec((kb, d), lambda i: (0, 0)),
                  pl.BlockSpec((1, d), lambda i: (0, 0))],
        out_specs=[pl.BlockSpec((tm, d), lambda i: (i, 0)),
                   pl.BlockSpec((d, tm), lambda i: (0, i))],
        out_shape=[jax.ShapeDtypeStruct((m, d), jnp.float32),
                   jax.ShapeDtypeStruct((d, m), jnp.bfloat16)],
        compiler_params=_cparams(("parallel",)),
        name="outproj",
    )(x2d, ya, yb, wo_a, wo_b, g_ffn)


def _sort_network(n):
    def merge(lo, hi, r):
        step = r * 2
        if step < hi - lo:
            yield from merge(lo, hi, step)
            yield from merge(lo + r, hi, step)
            yield from [(i, i + r) for i in range(lo + r, hi - r, step)]
        else:
            yield (lo, lo + r)

    def sort(lo, hi):
        if hi - lo >= 1:
            mid = lo + (hi - lo) // 2
            yield from sort(lo, mid)
            yield from sort(mid + 1, hi)
            yield from merge(lo, hi, 1)

    return list(sort(0, n - 1))


SUBLANES = 8


def _top_values(x, k):
    ngroups = x.shape[0] // SUBLANES
    rows = [x[SUBLANES * i:SUBLANES * (i + 1)] for i in range(ngroups)]
    for i, j in _sort_network(ngroups):
        rows[i], rows[j] = jnp.maximum(rows[i], rows[j]), jnp.minimum(rows[i], rows[j])
    out = []
    for r in range(k):
        m = jnp.max(rows[0], axis=0, keepdims=True)
        out.append(m)
        live = min(ngroups, k - 1 - r)
        pop = rows[0] == m
        for lvl in range(live):
            below = rows[lvl + 1] if lvl + 1 < ngroups else jnp.full_like(rows[lvl], -jnp.inf)
            rows[lvl] = jnp.where(pop, below, rows[lvl])
    return out


_N_TOP = PEER_TOPK + 1
_CAND_PAIRS = [(a, b) for a in range(_N_TOP) for b in range(_N_TOP) if (a + 1) * (b + 1) <= _N_TOP]


def _peer_score_kernel(h_ref, wq_ref, sk_ref, b_ref, beta_ref, a_ref):
    tm = h_ref.shape[1]
    qr_t = jnp.dot(wq_ref[...], h_ref[...], preferred_element_type=jnp.float32)
    for h in range(PEER_HEADS):
        q0 = qr_t[(2 * h) * PEER_NKEYS:(2 * h + 1) * PEER_NKEYS].astype(jnp.bfloat16)
        q1 = qr_t[(2 * h + 1) * PEER_NKEYS:(2 * h + 2) * PEER_NKEYS].astype(jnp.bfloat16)
        s0 = jnp.dot(sk_ref[2 * h], q0, preferred_element_type=jnp.float32)
        s1 = jnp.dot(sk_ref[2 * h + 1], q1, preferred_element_type=jnp.float32)
        top0 = _top_values(s0, _N_TOP)
        top1 = _top_values(s1, _N_TOP)
        cands = [top0[a] + top1[b] for (a, b) in _CAND_PAIRS]
        pad = (-len(cands)) % (8 * SUBLANES)
        cands += [jnp.full((1, tm), -jnp.inf, jnp.float32)] * pad
        vals = _top_values(jnp.concatenate(cands, axis=0), _N_TOP)
        z = jnp.zeros((1, tm), jnp.float32)
        for v in vals[:PEER_TOPK]:
            z = z + jnp.exp(v - vals[0])
        cut = 0.5 * (vals[PEER_TOPK - 1] + vals[PEER_TOPK])
        b_ref[h] = jnp.exp(s1 - top1[0]).astype(b_ref.dtype)
        beta_ref[h] = jnp.exp((cut - top1[0]) - s0)
        a_ref[h] = jnp.exp(s0 - top0[0]) / z


def _peer_scores(h2t, wpq_t, sk, tm=256):
    d, m = h2t.shape
    nq = wpq_t.shape[0]
    out_spec = pl.BlockSpec((PEER_HEADS, PEER_NKEYS, tm), lambda i: (0, 0, i))
    out_shape = jax.ShapeDtypeStruct((PEER_HEADS, PEER_NKEYS, m), jnp.float32)
    return pl.pallas_call(
        _peer_score_kernel,
        grid=(m // tm,),
        in_specs=[pl.BlockSpec((d, tm), lambda i: (0, i)),
                  pl.BlockSpec((nq, d), lambda i: (0, 0)),
                  pl.BlockSpec(sk.shape, lambda i: (0, 0, 0))],
        out_specs=[out_spec, out_spec, out_spec],
        out_shape=[jax.ShapeDtypeStruct(out_shape.shape, jnp.bfloat16), out_shape, out_shape],
        compiler_params=_cparams(("parallel",)),
        name="peer_scores",
    )(h2t, wpq_t, sk)


def _gelu_exact(x):
    return 0.5 * x * (1.0 + lax.erf(x * (1.0 / math.sqrt(2.0))))


PEER_KC = 256


def _peer_main_step(h_ref, u_ref, vt_ref, b_ref, beta_ref, a_ref, acc_scr, w_scr, act_next, act_cur):
    tm = h_ref.shape[1]
    te = u_ref.shape[0]
    if act_next is not None:
        act_next[...] = jnp.dot(u_ref[...], h_ref[...], preferred_element_type=jnp.float32)
    if act_cur is None:
        return
    parts = []
    for c in range(te // PEER_KC):
        for il in range(c * PEER_KC // PEER_NKEYS, (c + 1) * PEER_KC // PEER_NKEYS):
            rows = slice(il * PEER_NKEYS, (il + 1) * PEER_NKEYS)
            for lt in range(tm // LANES):
                lanes = slice(lt * LANES, (lt + 1) * LANES)
                g = None
                for h in range(PEER_HEADS):
                    b = b_ref[h, :, lanes]
                    beta = beta_ref[h, il:il + 1, lanes].astype(jnp.bfloat16)
                    a = a_ref[h, il:il + 1, lanes].astype(jnp.bfloat16)
                    term = a * jnp.where(b >= beta, b, jnp.zeros_like(b))
                    g = term if g is None else g + term
                w_scr[rows, lanes] = g * _gelu_exact(act_cur[rows, lanes]).astype(jnp.bfloat16)
        ck = slice(c * PEER_KC, (c + 1) * PEER_KC)
        parts.append(jnp.dot(vt_ref[:, ck], w_scr[ck, :], preferred_element_type=jnp.float32))
    acc_scr[...] += functools.reduce(lambda p, q: p + q, parts)


def _peer_main_kernel(last, h_ref, u_ref, vt_ref, b_ref, beta_ref, a_ref, x1_ref, o_ref,
                      acc_scr, w_scr, act_a, act_b):
    e = pl.program_id(1)
    args = (h_ref, u_ref, vt_ref, b_ref, beta_ref, a_ref, acc_scr, w_scr)
    slots = (act_a, act_b)

    @pl.when(e == 0)
    def _():
        acc_scr[...] = jnp.zeros(acc_scr.shape, jnp.float32)
        _peer_main_step(*args, slots[0], None)

    for parity in range(2):
        @pl.when(jnp.logical_and(e % 2 == parity, jnp.logical_and(e > 0, e < last)))
        def _():
            _peer_main_step(*args, slots[parity], slots[1 - parity])

    @pl.when(e == last)
    def _():
        _peer_main_step(*args, None, slots[(last - 1) % 2])
        o_ref[...] = x1_ref[...] + acc_scr[...].T


def _peer_main(h2t, u_bf, vt_bf, bs, betas, a_s, x1, tm=512, te=1024):
    d, m = h2t.shape
    ne = u_bf.shape[0] // te
    ni = te // PEER_NKEYS
    back = lambda e, n: jnp.clip(e - n, 0, ne - 1)
    once = pl.Buffered(1)
    return pl.pallas_call(
        functools.partial(_peer_main_kernel, ne),
        grid=(m // tm, ne + 1),
        in_specs=[pl.BlockSpec((d, tm), lambda i, e: (0, i), pipeline_mode=once),
                  pl.BlockSpec((te, d), lambda i, e: (back(e, 0), 0)),
                  pl.BlockSpec((d, te), lambda i, e: (0, back(e, 1))),
                  pl.BlockSpec((PEER_HEADS, PEER_NKEYS, tm), lambda i, e: (0, 0, i), pipeline_mode=once),
                  pl.BlockSpec((PEER_HEADS, ni, tm), lambda i, e: (0, back(e, 1), i)),
                  pl.BlockSpec((PEER_HEADS, ni, tm), lambda i, e: (0, back(e, 1), i)),
                  pl.BlockSpec((tm, d), lambda i, e: (i, 0), pipeline_mode=once)],
        out_specs=pl.BlockSpec((tm, d), lambda i, e: (i, 0)),
        out_shape=jax.ShapeDtypeStruct((m, d), jnp.float32),
        scratch_shapes=[pltpu.VMEM((d, tm), jnp.float32),
                        pltpu.VMEM((te, tm), jnp.bfloat16),
                        pltpu.VMEM((te, tm), jnp.float32),
                        pltpu.VMEM((te, tm), jnp.float32)],
        compiler_params=_cparams(("parallel", "arbitrary")),
        name="peer_main",
    )(h2t, u_bf, vt_bf, bs, betas, a_s, x1)


def _rope_tables(s):
    half = MLA_ROPE // 2
    freqs = ROPE_THETA ** (-2.0 * jnp.arange(half, dtype=jnp.float32) / MLA_ROPE)
    ang = jnp.arange(s, dtype=jnp.float32)[:, None] * freqs[None, :]
    cos, sin = jnp.cos(ang), jnp.sin(ang)
    z = jnp.zeros_like(cos)
    c = jnp.concatenate([cos, cos, z, z], axis=1)
    s1 = jnp.concatenate([z, sin, z, z], axis=1)
    s2 = jnp.concatenate([-sin, z, z, z], axis=1)
    return c, s1, s2


def _mla_mask_table():
    t = ATT_T
    r = np.arange(t)[:, None] // CHUNK
    c = np.arange(t)[None, :] // CHUNK
    return jnp.asarray(np.where(c <= r, 0.0, NEG).astype(np.float32))


def _layer(x, g_attn, w_in, g_cq, g_ckv, w_uq, w_ukv, g_mla_q, g_mla_k, g_dsa_q, g_dsa_k,
           rel_bias, w_o, g_ffn, w_pq, sub_keys, u_tab, v_tab):
    b, s, d = x.shape
    m = b * s
    bf = jnp.bfloat16
    f32 = jnp.float32
    assert s % ATT_T == 0 and m % 512 == 0

    o = np.cumsum([0, Q_LORA, KV_LORA, MLA_ROPE, 1024, 1024, 1024, 1024, IDX_HD, IDX_HEADS])
    zc = lambda n: jnp.zeros((d, n), bf)
    w_in_b = w_in.astype(bf)
    w_in_p = jnp.concatenate([
        w_in_b[:, o[3]:o[7]], w_in_b[:, o[0]:o[2]],
        w_in_b[:, o[2]:o[3]], zc(LANES - MLA_ROPE),
        w_in_b[:, o[7]:o[8]], zc(LANES - IDX_HD),
        w_in_b[:, o[8]:o[9]], zc(LANES - IDX_HEADS), zc(IN_PAD - COL_WI - LANES)], axis=1)
    wuq_p = jnp.pad(w_uq.reshape(Q_LORA, MLA_HEADS, MLA_QK),
                    ((0, 0), (0, 0), (0, MLA_QK_PAD - MLA_QK))).reshape(Q_LORA, -1).astype(bf)
    wukv_p = w_ukv.reshape(KV_LORA, MLA_HEADS, 2, MLA_NOPE).transpose(0, 2, 1, 3).reshape(KV_LORA, -1).astype(bf)
    gq_pad = jnp.pad(g_mla_q, (0, MLA_QK_PAD - MLA_QK))[None]
    gk_pad = jnp.pad(g_mla_k, (0, MLA_QK_PAD - MLA_QK))[None]
    rope_c, rope_s1, rope_s2 = _rope_tables(s)

    proj = _inproj(x.reshape(m, d), g_attn[None], w_in_p)
    proj3 = proj.reshape(b, s, IN_PAD)

    q_a, k_a, v_a = _mla_prep(proj3, g_cq[None], g_ckv[None], wuq_p, wukv_p, gq_pad, gk_pad,
                              rope_c, rope_s1, rope_s2)
    y_a = _mla_attn(q_a, k_a, v_a, _mla_mask_table())

    q_d, k_d, v_d, q_i, k_i2 = _dsa_prep(proj3, g_dsa_q[None], g_dsa_k[None])
    n_sel = min(DSA_TOPK, s // 4)
    sel = _dsa_index(q_i, proj3, k_i2, n_sel, COL_WI // LANES)
    y_b = _dsa_attn(q_d, k_d, v_d, sel, _bias_tiles(rel_bias))

    na = MLA_HEADS * MLA_V
    x1, h2t = _outproj(x.reshape(m, d), y_a.reshape(m, na), y_b.reshape(m, -1),
                      w_o[:na].astype(bf), w_o[na:].astype(bf), g_ffn[None])

    sk = sub_keys.reshape(PEER_HEADS * 2, PEER_NKEYS, -1).astype(bf)
    bs, betas, a_s = _peer_scores(h2t, w_pq.T.astype(bf), sk)
    x2 = _peer_main(h2t, u_tab.astype(bf), v_tab.T.astype(bf), bs, betas, a_s, x1)
    return x2.reshape(b, s, d)


def kernel(x, g_attn, w_in, g_cq, g_ckv, w_uq, w_ukv, g_mla_q, g_mla_k, g_dsa_q, g_dsa_k,
           rel_bias, w_o, g_ffn, w_pq, sub_keys, u_tab, v_tab):
    for l in range(w_in.shape[0]):
        x = _layer(x, g_attn[l], w_in[l], g_cq[l], g_ckv[l], w_uq[l], w_ukv[l], g_mla_q[l], g_mla_k[l],
                   g_dsa_q[l], g_dsa_k[l], rel_bias, w_o[l], g_ffn[l], w_pq[l], sub_keys[l],
                   u_tab[l], v_tab[l])
    return x
```

```python
import functools
import math

import numpy as np
import jax
import jax.numpy as jnp
from jax import lax
from jax.experimental import pallas as pl
from jax.experimental.pallas import tpu as pltpu

EPS = 1e-6
CHUNK = 64

MLA_HEADS = 8
MLA_NOPE = 128
MLA_ROPE = 64
MLA_V = 128
MLA_QK = MLA_NOPE + MLA_ROPE
MLA_QK_PAD = 256
Q_LORA = 512
KV_LORA = 256
ROPE_THETA = 10000.0

DSA_HEADS = 8
DSA_HD = 128
IDX_HEADS = 16
IDX_HD = 64
DSA_TOPK = 256

REL_BUCKETS = 32
REL_MAX_DIST = 128

PEER_HEADS = 8
PEER_NKEYS = 128
PEER_TOPK = 16

LANES = 128
NEG = -1e30
LOG2E = math.log2(math.e)
VMEM_LIMIT = 56 * 1024 * 1024

COL_Q, COL_K, COL_V, COL_QI = 0, 1024, 2048, 3072
COL_CQ, COL_CKV, COL_KR, COL_KI, COL_WI = 4096, 4608, 4864, 4992, 5120
IN_PAD = 5376

ATT_T = 256

_NT = (((1,), (1,)), ((), ()))


def _cparams(sem):
    return pltpu.CompilerParams(dimension_semantics=sem, vmem_limit_bytes=VMEM_LIMIT)


def _rep(x, n):
    return x if n == 1 else jnp.concatenate([x] * n, axis=1)


def _inproj_kernel(x_ref, g_ref, w_ref, o_ref, h_scr):
    @pl.when(pl.program_id(1) == 0)
    def _():
        x = x_ref[...]
        ms = jnp.mean(x * x, axis=-1, keepdims=True)
        h_scr[...] = (x * lax.rsqrt(ms + EPS) * g_ref[...]).astype(jnp.bfloat16)

    o_ref[...] = jnp.dot(h_scr[...], w_ref[...], preferred_element_type=jnp.float32)


def _inproj(x2d, g, w_p, tm=1024, tn=768):
    m, d = x2d.shape
    tm = min(tm, m)
    n = w_p.shape[1]
    return pl.pallas_call(
        _inproj_kernel,
        grid=(m // tm, n // tn),
        in_specs=[pl.BlockSpec((tm, d), lambda i, j: (i, 0)),
                  pl.BlockSpec((1, d), lambda i, j: (0, 0)),
                  pl.BlockSpec((d, tn), lambda i, j: (0, j))],
        out_specs=pl.BlockSpec((tm, tn), lambda i, j: (i, j)),
        out_shape=jax.ShapeDtypeStruct((m, n), jnp.float32),
        scratch_shapes=[pltpu.VMEM((tm, d), jnp.bfloat16)],
        compiler_params=_cparams(("parallel", "arbitrary")),
        name="inproj",
    )(x2d, g, w_p)


def _rope128(xr, c, s1, s2):
    return xr * c + pltpu.roll(xr, 96, 1) * s2 + pltpu.roll(xr, 32, 1) * s1


def _mla_prep_kernel(cq_ref, ckv_ref, kr_ref, gcq_ref, gckv_ref, wuq_ref, wukv_ref, gq_ref, gk_ref,
                     c_ref, s1_ref, s2_ref, q_ref, k_ref, v_ref):
    cq = cq_ref[0]
    cqn = (cq * lax.rsqrt(jnp.mean(cq * cq, axis=-1, keepdims=True) + EPS) * gcq_ref[...])
    q_all = jnp.dot(cqn.astype(jnp.bfloat16), wuq_ref[...], preferred_element_type=jnp.float32)
    ckv = ckv_ref[0]
    ckvn = (ckv * lax.rsqrt(jnp.mean(ckv * ckv, axis=-1, keepdims=True) + EPS) * gckv_ref[...])
    kv = jnp.dot(ckvn.astype(jnp.bfloat16), wukv_ref[...], preferred_element_type=jnp.float32)
    kr = kr_ref[0]
    kr_ss = jnp.sum(kr * kr, axis=-1, keepdims=True)
    c, s1, s2 = c_ref[...], s1_ref[...], s2_ref[...]
    gq, gk = gq_ref[...], gk_ref[...]
    scale = MLA_QK ** -0.5 * LOG2E
    for h in range(MLA_HEADS):
        qh = q_all[:, h * MLA_QK_PAD:(h + 1) * MLA_QK_PAD]
        rq = lax.rsqrt(jnp.sum(qh * qh, axis=-1, keepdims=True) * (1.0 / MLA_QK) + EPS)
        qn = qh * rq * gq
        q_ref[0, h, :, :MLA_NOPE] = (qn[:, :MLA_NOPE] * scale).astype(jnp.bfloat16)
        q_ref[0, h, :, MLA_NOPE:] = (_rope128(qn[:, MLA_NOPE:], c, s1, s2) * scale).astype(jnp.bfloat16)
        kn = kv[:, h * MLA_NOPE:(h + 1) * MLA_NOPE]
        rk = lax.rsqrt((jnp.sum(kn * kn, axis=-1, keepdims=True) + kr_ss) * (1.0 / MLA_QK) + EPS)
        k_ref[0, h, :, :MLA_NOPE] = (kn * rk * gk[:, :MLA_NOPE]).astype(jnp.bfloat16)
        k_ref[0, h, :, MLA_NOPE:] = _rope128(kr * rk * gk[:, MLA_NOPE:], c, s1, s2).astype(jnp.bfloat16)
        v_ref[0, h] = kv[:, MLA_HEADS * MLA_NOPE + h * MLA_V:MLA_HEADS * MLA_NOPE + (h + 1) * MLA_V
                         ].astype(jnp.bfloat16)


def _mla_prep(proj3, g_cq, g_ckv, wuq_p, wukv_p, gq_pad, gk_pad, rope_c, rope_s1, rope_s2, tm=256):
    b, s, _ = proj3.shape
    const = lambda shape: pl.BlockSpec(shape, lambda bi, i: (0,) * len(shape))
    return pl.pallas_call(
        _mla_prep_kernel,
        grid=(b, s // tm),
        in_specs=[pl.BlockSpec((1, tm, Q_LORA), lambda bi, i: (bi, i, COL_CQ // Q_LORA)),
                  pl.BlockSpec((1, tm, KV_LORA), lambda bi, i: (bi, i, COL_CKV // KV_LORA)),
                  pl.BlockSpec((1, tm, LANES), lambda bi, i: (bi, i, COL_KR // LANES)),
                  const((1, Q_LORA)), const((1, KV_LORA)),
                  const(wuq_p.shape), const(wukv_p.shape),
                  const((1, MLA_QK_PAD)), const((1, MLA_QK_PAD)),
                  pl.BlockSpec((tm, LANES), lambda bi, i: (i, 0)),
                  pl.BlockSpec((tm, LANES), lambda bi, i: (i, 0)),
                  pl.BlockSpec((tm, LANES), lambda bi, i: (i, 0))],
        out_specs=[pl.BlockSpec((1, MLA_HEADS, tm, MLA_QK_PAD), lambda bi, i: (bi, 0, i, 0)),
                   pl.BlockSpec((1, MLA_HEADS, tm, MLA_QK_PAD), lambda bi, i: (bi, 0, i, 0)),
                   pl.BlockSpec((1, MLA_HEADS, tm, MLA_V), lambda bi, i: (bi, 0, i, 0))],
        out_shape=[jax.ShapeDtypeStruct((b, MLA_HEADS, s, MLA_QK_PAD), jnp.bfloat16),
                   jax.ShapeDtypeStruct((b, MLA_HEADS, s, MLA_QK_PAD), jnp.bfloat16),
                   jax.ShapeDtypeStruct((b, MLA_HEADS, s, MLA_V), jnp.bfloat16)],
        compiler_params=_cparams(("parallel", "parallel")),
        name="mla_prep",
    )(proj3, proj3, proj3, g_cq, g_ckv, wuq_p, wukv_p, gq_pad, gk_pad, rope_c, rope_s1, rope_s2)


def _dsa_prep_kernel(q_ref, k_ref, v_ref, qi_ref, ki_ref, gq_ref, gk_ref,
                     qo_ref, ko_ref, vo_ref, qio_ref, kio_ref):
    gq, gk = gq_ref[...], gk_ref[...]
    scale = DSA_HD ** -0.5 * LOG2E
    for h in range(DSA_HEADS):
        sl = slice(h * DSA_HD, (h + 1) * DSA_HD)
        qh = q_ref[0, :, sl]
        qo_ref[0, h] = (qh * lax.rsqrt(jnp.mean(qh * qh, axis=-1, keepdims=True) + EPS) * gq * scale
                        ).astype(jnp.bfloat16)
        kh = k_ref[0, :, sl]
        ko_ref[0, h] = (kh * lax.rsqrt(jnp.mean(kh * kh, axis=-1, keepdims=True) + EPS) * gk
                        ).astype(jnp.bfloat16)
        vo_ref[0, h] = v_ref[0, :, sl].astype(jnp.bfloat16)
    qio_ref[0] = (qi_ref[0] * (IDX_HD ** -0.5)).astype(jnp.bfloat16)
    ki = ki_ref[0]
    kio_ref[0, 0] = ki.astype(jnp.bfloat16)
    kio_ref[0, 1] = pltpu.roll(ki, IDX_HD, 1).astype(jnp.bfloat16)


def _dsa_prep(proj3, gq, gk, tm=256):
    b, s, _ = proj3.shape
    w = DSA_HEADS * DSA_HD
    hd_spec = pl.BlockSpec((1, DSA_HEADS, tm, DSA_HD), lambda bi, i: (bi, 0, i, 0))
    hd_shape = jax.ShapeDtypeStruct((b, DSA_HEADS, s, DSA_HD), jnp.bfloat16)
    return pl.pallas_call(
        _dsa_prep_kernel,
        grid=(b, s // tm),
        in_specs=[pl.BlockSpec((1, tm, w), lambda bi, i: (bi, i, COL_Q // w)),
                  pl.BlockSpec((1, tm, w), lambda bi, i: (bi, i, COL_K // w)),
                  pl.BlockSpec((1, tm, w), lambda bi, i: (bi, i, COL_V // w)),
                  pl.BlockSpec((1, tm, w), lambda bi, i: (bi, i, COL_QI // w)),
                  pl.BlockSpec((1, tm, LANES), lambda bi, i: (bi, i, COL_KI // LANES)),
                  pl.BlockSpec((1, DSA_HD), lambda bi, i: (0, 0)),
                  pl.BlockSpec((1, DSA_HD), lambda bi, i: (0, 0))],
        out_specs=[hd_spec, hd_spec, hd_spec,
                   pl.BlockSpec((1, tm, w), lambda bi, i: (bi, i, 0)),
                   pl.BlockSpec((1, 2, tm, LANES), lambda bi, i: (bi, 0, i, 0))],
        out_shape=[hd_shape, hd_shape, hd_shape,
                   jax.ShapeDtypeStruct((b, s, w), jnp.bfloat16),
                   jax.ShapeDtypeStruct((b, 2, s, LANES), jnp.bfloat16)],
        compiler_params=_cparams(("parallel", "parallel")),
        name="dsa_prep",
    )(proj3, proj3, proj3, proj3, proj3, gq, gk)


def _lane_fold(x, op):
    out = x[:, :LANES]
    for c in range(1, x.shape[1] // LANES):
        out = op(out, x[:, c * LANES:(c + 1) * LANES])
    return out


ATT_SLOTS = 8


def _attn_scratch(nsub, n_tiles):
    t = ATT_T
    return ([pltpu.VMEM((nsub, t, LANES), jnp.float32)] * 3
            + [pltpu.VMEM((ATT_SLOTS, t, t), jnp.float32), pltpu.VMEM((ATT_SLOTS, t, t), jnp.bfloat16),
               pltpu.VMEM((nsub * n_tiles, t, t), jnp.float32)])


def _attn_sweep(q_ref, k_ref, v_ref, o_ref, m_scr, l_scr, acc_scr, s_scr, p_scr, all_scr, add_fn):
    qi = pl.program_id(2)
    t = ATT_T
    nsub = q_ref.shape[2] // t
    nrep = t // LANES
    nslot = s_scr.shape[0]
    first = nsub * qi

    def qk(slot, unit):
        r, ki, _ = unit
        off = pl.multiple_of(ki * t, t)
        s_scr[slot] = lax.dot_general(q_ref[0, 0, r * t:(r + 1) * t, :], k_ref[0, 0, pl.ds(off, t), :],
                                      _NT, preferred_element_type=jnp.float32)

    n_tiles = all_scr.shape[0] // nsub

    def fold_max(slot, unit):
        r, ki, _ = unit
        a = add_fn(*unit)
        s = s_scr[slot] if a is None else s_scr[slot] + a
        all_scr[r * n_tiles + ki] = s
        m_scr[r] = jnp.maximum(m_scr[r], _lane_fold(s, jnp.maximum))

    def probs(slot, unit):
        r, ki, _ = unit
        p = jnp.exp2(all_scr[r * n_tiles + ki] - _rep(m_scr[r], nrep))
        l_scr[r] += _lane_fold(p, jnp.add)
        p_scr[slot] = p.astype(jnp.bfloat16)

    def pv(slot, unit):
        r, ki, _ = unit
        off = pl.multiple_of(ki * t, t)
        acc_scr[r] += jnp.dot(p_scr[slot], v_ref[0, 0, pl.ds(off, t), :], preferred_element_type=jnp.float32)

    def run(units, stages):
        n = len(units)
        for step in range(n + len(stages) - 1):
            for k, stage in enumerate(stages):
                i = step - k
                if 0 <= i < n:
                    stage(i % nslot, units[i])

    def sweep(stages):
        def body(kp, carry):
            run([(r, 2 * kp + kk, None) for kk in range(2) for r in range(nsub)], stages)
            return carry

        if nsub % 2 == 0:
            lax.fori_loop(0, first // 2, body, 0)
        else:
            lax.fori_loop(0, first, lambda ki, c: (run([(0, ki, None)], stages), c)[1], 0)
        run([(r, first + j, j - r) for j in range(nsub) for r in range(j, nsub)], stages)

    m_scr[...] = jnp.full(m_scr.shape, NEG, jnp.float32)
    sweep((qk, fold_max))
    for r in range(nsub):
        m_scr[r] = jnp.broadcast_to(jnp.max(m_scr[r], axis=1, keepdims=True), (t, LANES))
    l_scr[...] = jnp.zeros(l_scr.shape, jnp.float32)
    acc_scr[...] = jnp.zeros(acc_scr.shape, jnp.float32)
    sweep((probs, pv))
    for r in range(nsub):
        l = jnp.sum(l_scr[r], axis=1, keepdims=True)
        o_ref[0, r * t:(r + 1) * t, :] = (acc_scr[r] / l).astype(o_ref.dtype)


def _attn_nsub(n_tiles):
    return 4 if n_tiles % 4 == 0 else (2 if n_tiles % 2 == 0 else 1)


def _mla_attn_kernel(q_ref, k_ref, v_ref, mask_ref, o_ref, *scr):
    def add_fn(r, ki, rel):
        return mask_ref[...] if rel == 0 else None

    _attn_sweep(q_ref, k_ref, v_ref, o_ref, *scr, add_fn)


def _mla_attn(q, k, v, mask_tbl):
    b, h, s, dq = q.shape
    t = ATT_T
    nsub = _attn_nsub(s // t)
    tq = nsub * t
    return pl.pallas_call(
        _mla_attn_kernel,
        grid=(b, h, s // tq),
        in_specs=[pl.BlockSpec((1, 1, tq, dq), lambda bi, hi, i: (bi, hi, i, 0)),
                  pl.BlockSpec((1, 1, s, dq), lambda bi, hi, i: (bi, hi, 0, 0)),
                  pl.BlockSpec((1, 1, s, MLA_V), lambda bi, hi, i: (bi, hi, 0, 0)),
                  pl.BlockSpec((t, t), lambda bi, hi, i: (0, 0))],
        out_specs=pl.BlockSpec((1, tq, MLA_V), lambda bi, hi, i: (bi, i, hi)),
        out_shape=jax.ShapeDtypeStruct((b, s, h * MLA_V), jnp.bfloat16),
        scratch_shapes=_attn_scratch(nsub, s // t),
        compiler_params=_cparams(("parallel", "parallel", "arbitrary")),
        name="mla_attn",
    )(q, k, v, mask_tbl)


def _dsa_attn_kernel(q_ref, k_ref, v_ref, sel_ref, bias_ref, o_ref, *scr):
    nsub = q_ref.shape[2] // ATT_T
    first = nsub * pl.program_id(2)

    def add_fn(r, ki, rel):
        back = jnp.minimum(first + r - ki, 2) if rel is None else min(-rel, 2)
        return bias_ref[0, back] + sel_ref[0, r, ki].astype(jnp.float32)

    _attn_sweep(q_ref, k_ref, v_ref, o_ref, *scr, add_fn)


def _dsa_attn(q, k, v, sel, bias_tbl):
    b, h, s, d = q.shape
    t = ATT_T
    n = s // t
    nsub = _attn_nsub(n)
    tq = nsub * t
    return pl.pallas_call(
        _dsa_attn_kernel,
        grid=(b, h, s // tq),
        in_specs=[pl.BlockSpec((1, 1, tq, d), lambda bi, hi, i: (bi, hi, i, 0)),
                  pl.BlockSpec((1, 1, s, d), lambda bi, hi, i: (bi, hi, 0, 0)),
                  pl.BlockSpec((1, 1, s, d), lambda bi, hi, i: (bi, hi, 0, 0)),
                  pl.BlockSpec((1, nsub, n, t, t), lambda bi, hi, i: (bi, i, 0, 0, 0)),
                  pl.BlockSpec((1, 3, t, t), lambda bi, hi, i: (hi, 0, 0, 0))],
        out_specs=pl.BlockSpec((1, tq, d), lambda bi, hi, i: (bi, i, hi)),
        out_shape=jax.ShapeDtypeStruct((b, s, h * d), jnp.bfloat16),
        scratch_shapes=_attn_scratch(nsub, s // t),
        compiler_params=_cparams(("parallel", "parallel", "arbitrary")),
        name="dsa_attn",
    )(q, k, v, sel, bias_tbl)


def _bias_kernel(rb_ref, o_ref):
    h = pl.program_id(0)
    t = o_ref.shape[2]
    row = lax.broadcasted_iota(jnp.int32, (t, t), 0)
    col = lax.broadcasted_iota(jnp.int32, (t, t), 1)
    nb = REL_BUCKETS // 2
    max_exact = nb // 2
    for r in range(2):
        rel = col - row - r * t
        n = jnp.abs(rel)
        n2 = n * n
        large = jnp.full((t, t), max_exact, jnp.int32)
        for j in range(1, nb - max_exact):
            large = large + (n2 >= (max_exact * max_exact) * (2 ** j)).astype(jnp.int32)
        bucket = jnp.where(rel > 0, nb, 0) + jnp.where(n < max_exact, n, large)
        val = jnp.zeros((t, t), jnp.float32)
        for bkt in range(REL_BUCKETS):
            val = jnp.where(bucket == bkt, rb_ref[bkt, h], val)
        o_ref[0, r] = val * LOG2E
    o_ref[0, 2] = jnp.full((t, t), rb_ref[nb - 1, h] * LOG2E, jnp.float32)


def _bias_tiles(rel_bias):
    t = ATT_T
    assert t >= REL_MAX_DIST
    return pl.pallas_call(
        _bias_kernel,
        grid=(DSA_HEADS,),
        in_specs=[pl.BlockSpec(memory_space=pltpu.SMEM)],
        out_specs=pl.BlockSpec((1, 3, t, t), lambda h: (h, 0, 0, 0)),
        out_shape=jax.ShapeDtypeStruct((DSA_HEADS, 3, t, t), jnp.float32),
        compiler_params=_cparams(("arbitrary",)),
        name="t5_bias",
    )(rel_bias)


_F32_INF_BITS = 0x7F800000
_INT_MIN = np.int32(-2 ** 31)


def _index_kernel(n_sel, q_ref, w_ref, k_ref, o_ref, score_scr, wt_scr):
    qi = pl.program_id(1)
    t = q_ref.shape[1]
    n_tiles = score_scr.shape[0]
    n_adm = qi + 1
    wt_scr[...] = w_ref[0].T
    q_chunk = (qi * t + lax.broadcasted_iota(jnp.int32, (1, t), 1)) // CHUNK

    def score_body(c, carry):
        off = pl.multiple_of(c * t, t)
        acc = jnp.zeros((t, t), jnp.float32)
        for hp in range(IDX_HEADS // 2):
            qpair = q_ref[0, :, hp * LANES:(hp + 1) * LANES]
            for par in range(2):
                h = 2 * hp + par
                kk = k_ref[0, par, pl.ds(off, t), :]
                d = lax.dot_general(kk, qpair, _NT, preferred_element_type=jnp.float32)
                acc = acc + jnp.maximum(d, 0.0) * wt_scr[h:h + 1, :]
        k_chunk = (c * t + lax.broadcasted_iota(jnp.int32, (t, 1), 0)) // CHUNK
        score_scr[c] = jnp.where(k_chunk <= q_chunk, acc, -jnp.inf)
        return carry

    lax.fori_loop(0, n_adm, score_body, 0)

    def count_ge(cand):
        def cnt_body(c, cnt):
            ge = jnp.where(score_scr[c] >= cand, 1.0, 0.0)
            return cnt + jnp.sum(ge.reshape(4, t // 32, 8, t), axis=1)

        cnt = lax.fori_loop(0, n_adm, cnt_body, jnp.zeros((4, 8, t), jnp.float32))
        return jnp.sum(jnp.sum(cnt, axis=0), axis=0, keepdims=True)

    kf = float(n_sel)
    pos = count_ge(jnp.zeros((1, t), jnp.float32)) >= kf

    def thr_of(u):
        neg_bits = jnp.bitwise_or(_F32_INF_BITS - u, _INT_MIN)
        return pltpu.bitcast(jnp.where(pos, u, neg_bits), jnp.float32)

    def bit_body(it, u):
        cand = u + jnp.left_shift(jnp.int32(1), 30 - it)
        ok = jnp.logical_and(cand <= _F32_INF_BITS, count_ge(thr_of(cand)) >= kf)
        return jnp.where(ok, cand, u)

    thr = thr_of(lax.fori_loop(0, 31, bit_body, jnp.zeros((1, t), jnp.int32)))

    def out_body(c, carry):
        score = score_scr[c]
        sel = jnp.logical_and(score >= thr, score > -jnp.inf)
        o_ref[0, 0, c] = jnp.where(sel, 0.0, NEG).T.astype(o_ref.dtype)
        return carry

    lax.fori_loop(0, n_adm, out_body, 0)

    def fill_body(c, carry):
        o_ref[0, 0, c] = jnp.full((t, t), NEG, o_ref.dtype)
        return carry

    lax.fori_loop(n_adm, n_tiles, fill_body, 0)


def _dsa_index(q_idx, w3, k_idx2, n_sel, w_col_block):
    b, s, _ = q_idx.shape
    t = ATT_T
    n = s // t
    return pl.pallas_call(
        functools.partial(_index_kernel, n_sel),
        grid=(b, n),
        in_specs=[pl.BlockSpec((1, t, IDX_HEADS * IDX_HD), lambda bi, i: (bi, i, 0)),
                  pl.BlockSpec((1, t, LANES), lambda bi, i: (bi, i, w_col_block)),
                  pl.BlockSpec((1, 2, s, LANES), lambda bi, i: (bi, 0, 0, 0))],
        out_specs=pl.BlockSpec((1, 1, n, t, t), lambda bi, i: (bi, i, 0, 0, 0)),
        out_shape=jax.ShapeDtypeStruct((b, n, n, t, t), jnp.bfloat16),
        scratch_shapes=[pltpu.VMEM((n, t, t), jnp.float32),
                        pltpu.VMEM((LANES, t), jnp.float32)],
        compiler_params=_cparams(("parallel", "parallel")),
        name="dsa_index",
    )(q_idx, w3, k_idx2)


def _outproj_kernel(x_ref, ya_ref, yb_ref, wa_ref, wb_ref, g_ref, x1_ref, h2t_ref):
    y = (jnp.dot(ya_ref[...], wa_ref[...], preferred_element_type=jnp.float32)
         + jnp.dot(yb_ref[...], wb_ref[...], preferred_element_type=jnp.float32))
    x1 = x_ref[...] + y
    x1_ref[...] = x1
    ms = jnp.mean(x1 * x1, axis=-1, keepdims=True)
    h2t_ref[...] = (x1 * lax.rsqrt(ms + EPS) * g_ref[...]).T.astype(jnp.bfloat16)


def _outproj(x2d, ya, yb, wo_a, wo_b, g_ffn, tm=256):
    m, d = x2d.shape
    ka, kb = ya.shape[1], yb.shape[1]
    return pl.pallas_call(
        _outproj_kernel,
        grid=(m // tm,),
        in_specs=[pl.BlockSpec((tm, d), lambda i: (i, 0)),
                  pl.BlockSpec((tm, ka), lambda i: (i, 0)),
                  pl.BlockSpec((tm, kb), lambda i: (i, 0)),
                  pl.BlockSpec((ka, d), lambda i: (0, 0)),
                  pl.BlockSpec((kb, d), lambda i: (0, 0)),
                  pl.BlockSpec((1, d), lambda i: (0, 0))],
        out_specs=[pl.BlockSpec((tm, d), lambda i: (i, 0)),
                   pl.BlockSpec((d, tm), lambda i: (0, i))],
        out_shape=[jax.ShapeDtypeStruct((m, d), jnp.float32),
                   jax.ShapeDtypeStruct((d, m), jnp.bfloat16)],
        compiler_params=_cparams(("parallel",)),
        name="outproj",
    )(x2d, ya, yb, wo_a, wo_b, g_ffn)


def _sort_network(n):
    def merge(lo, hi, r):
        step = r * 2
        if step < hi - lo:
            yield from merge(lo, hi, step)
            yield from merge(lo + r, hi, step)
            yield from [(i, i + r) for i in range(lo + r, hi - r, step)]
        else:
            yield (lo, lo + r)

    def sort(lo, hi):
        if hi - lo >= 1:
            mid = lo + (hi - lo) // 2
            yield from sort(lo, mid)
            yield from sort(mid + 1, hi)
            yield from merge(lo, hi, 1)

    return list(sort(0, n - 1))


SUBLANES = 8


def _top_values(x, k):
    ngroups = x.shape[0] // SUBLANES
    rows = [x[SUBLANES * i:SUBLANES * (i + 1)] for i in range(ngroups)]
    for i, j in _sort_network(ngroups):
        rows[i], rows[j] = jnp.maximum(rows[i], rows[j]), jnp.minimum(rows[i], rows[j])
    out = []
    for r in range(k):
        m = jnp.max(rows[0], axis=0, keepdims=True)
        out.append(m)
        live = min(ngroups, k - 1 - r)
        pop = rows[0] == m
        for lvl in range(live):
            below = rows[lvl + 1] if lvl + 1 < ngroups else jnp.full_like(rows[lvl], -jnp.inf)
            rows[lvl] = jnp.where(pop, below, rows[lvl])
    return out


_N_TOP = PEER_TOPK + 1
_CAND_PAIRS = [(a, b) for a in range(_N_TOP) for b in range(_N_TOP) if (a + 1) * (b + 1) <= _N_TOP]


def _peer_score_kernel(h_ref, wq_ref, sk_ref, b_ref, beta_ref, a_ref):
    tm = h_ref.shape[1]
    qr_t = jnp.dot(wq_ref[...], h_ref[...], preferred_element_type=jnp.float32)
    for h in range(PEER_HEADS):
        q0 = qr_t[(2 * h) * PEER_NKEYS:(2 * h + 1) * PEER_NKEYS].astype(jnp.bfloat16)
        q1 = qr_t[(2 * h + 1) * PEER_NKEYS:(2 * h + 2) * PEER_NKEYS].astype(jnp.bfloat16)
        s0 = jnp.dot(sk_ref[2 * h], q0, preferred_element_type=jnp.float32)
        s1 = jnp.dot(sk_ref[2 * h + 1], q1, preferred_element_type=jnp.float32)
        top0 = _top_values(s0, _N_TOP)
        top1 = _top_values(s1, _N_TOP)
        cands = [top0[a] + top1[b] for (a, b) in _CAND_PAIRS]
        pad = (-len(cands)) % (8 * SUBLANES)
        cands += [jnp.full((1, tm), -jnp.inf, jnp.float32)] * pad
        vals = _top_values(jnp.concatenate(cands, axis=0), _N_TOP)
        z = jnp.zeros((1, tm), jnp.float32)
        for v in vals[:PEER_TOPK]:
            z = z + jnp.exp(v - vals[0])
        cut = 0.5 * (vals[PEER_TOPK - 1] + vals[PEER_TOPK])
        b_ref[h] = jnp.exp(s1 - top1[0]).astype(b_ref.dtype)
        beta_ref[h] = jnp.exp((cut - top1[0]) - s0)
        a_ref[h] = jnp.exp(s0 - top0[0]) / z


def _peer_scores(h2t, wpq_t, sk, tm=256):
    d, m = h2t.shape
    nq = wpq_t.shape[0]
    out_spec = pl.BlockSpec((PEER_HEADS, PEER_NKEYS, tm), lambda i: (0, 0, i))
    out_shape = jax.ShapeDtypeStruct((PEER_HEADS, PEER_NKEYS, m), jnp.float32)
    return pl.pallas_call(
        _peer_score_kernel,
        grid=(m // tm,),
        in_specs=[pl.BlockSpec((d, tm), lambda i: (0, i)),
                  pl.BlockSpec((nq, d), lambda i: (0, 0)),
                  pl.BlockSpec(sk.shape, lambda i: (0, 0, 0))],
        out_specs=[out_spec, out_spec, out_spec],
        out_shape=[jax.ShapeDtypeStruct(out_shape.shape, jnp.bfloat16), out_shape, out_shape],
        compiler_params=_cparams(("parallel",)),
        name="peer_scores",
    )(h2t, wpq_t, sk)


def _gelu_exact(x):
    return 0.5 * x * (1.0 + lax.erf(x * (1.0 / math.sqrt(2.0))))


PEER_KC = 256


def _peer_main_step(h_ref, u_ref, vt_ref, b_ref, beta_ref, a_ref, acc_scr, w_scr, act_next, act_cur):
    tm = h_ref.shape[1]
    te = u_ref.shape[0]
    if act_next is not None:
        act_next[...] = jnp.dot(u_ref[...], h_ref[...], preferred_element_type=jnp.float32)
    if act_cur is None:
        return
    parts = []
    for c in range(te // PEER_KC):
        for il in range(c * PEER_KC // PEER_NKEYS, (c + 1) * PEER_KC // PEER_NKEYS):
            rows = slice(il * PEER_NKEYS, (il + 1) * PEER_NKEYS)
            for lt in range(tm // LANES):
                lanes = slice(lt * LANES, (lt + 1) * LANES)
                g = None
                for h in range(PEER_HEADS):
                    b = b_ref[h, :, lanes]
                    beta = beta_ref[h, il:il + 1, lanes].astype(jnp.bfloat16)
                    a = a_ref[h, il:il + 1, lanes].astype(jnp.bfloat16)
                    term = a * jnp.where(b >= beta, b, jnp.zeros_like(b))
                    g = term if g is None else g + term
                w_scr[rows, lanes] = g * _gelu_exact(act_cur[rows, lanes]).astype(jnp.bfloat16)
        ck = slice(c * PEER_KC, (c + 1) * PEER_KC)
        parts.append(jnp.dot(vt_ref[:, ck], w_scr[ck, :], preferred_element_type=jnp.float32))
    acc_scr[...] += functools.reduce(lambda p, q: p + q, parts)


def _peer_main_kernel(last, h_ref, u_ref, vt_ref, b_ref, beta_ref, a_ref, x1_ref, o_ref,
                      acc_scr, w_scr, act_a, act_b):
    e = pl.program_id(1)
    args = (h_ref, u_ref, vt_ref, b_ref, beta_ref, a_ref, acc_scr, w_scr)
    slots = (act_a, act_b)

    @pl.when(e == 0)
    def _():
        acc_scr[...] = jnp.zeros(acc_scr.shape, jnp.float32)
        _peer_main_step(*args, slots[0], None)

    for parity in range(2):
        @pl.when(jnp.logical_and(e % 2 == parity, jnp.logical_and(e > 0, e < last)))
        def _():
            _peer_main_step(*args, slots[parity], slots[1 - parity])

    @pl.when(e == last)
    def _():
        _peer_main_step(*args, None, slots[(last - 1) % 2])
        o_ref[...] = x1_ref[...] + acc_scr[...].T


def _peer_main(h2t, u_bf, vt_bf, bs, betas, a_s, x1, tm=512, te=1024):
    d, m = h2t.shape
    ne = u_bf.shape[0] // te
    ni = te // PEER_NKEYS
    back = lambda e, n: jnp.clip(e - n, 0, ne - 1)
    once = pl.Buffered(1)
    return pl.pallas_call(
        functools.partial(_peer_main_kernel, ne),
        grid=(m // tm, ne + 1),
        in_specs=[pl.BlockSpec((d, tm), lambda i, e: (0, i), pipeline_mode=once),
                  pl.BlockSpec((te, d), lambda i, e: (back(e, 0), 0)),
                  pl.BlockSpec((d, te), lambda i, e: (0, back(e, 1))),
                  pl.BlockSpec((PEER_HEADS, PEER_NKEYS, tm), lambda i, e: (0, 0, i), pipeline_mode=once),
                  pl.BlockSpec((PEER_HEADS, ni, tm), lambda i, e: (0, back(e, 1), i)),
                  pl.BlockSpec((PEER_HEADS, ni, tm), lambda i, e: (0, back(e, 1), i)),
                  pl.BlockSpec((tm, d), lambda i, e: (i, 0), pipeline_mode=once)],
        out_specs=pl.BlockSpec((tm, d), lambda i, e: (i, 0)),
        out_shape=jax.ShapeDtypeStruct((m, d), jnp.float32),
        scratch_shapes=[pltpu.VMEM((d, tm), jnp.float32),
                        pltpu.VMEM((te, tm), jnp.bfloat16),
                        pltpu.VMEM((te, tm), jnp.float32),
                        pltpu.VMEM((te, tm), jnp.float32)],
        compiler_params=_cparams(("parallel", "arbitrary")),
        name="peer_main",
    )(h2t, u_bf, vt_bf, bs, betas, a_s, x1)


def _rope_tables(s):
    half = MLA_ROPE // 2
    freqs = ROPE_THETA ** (-2.0 * jnp.arange(half, dtype=jnp.float32) / MLA_ROPE)
    ang = jnp.arange(s, dtype=jnp.float32)[:, None] * freqs[None, :]
    cos, sin = jnp.cos(ang), jnp.sin(ang)
    z = jnp.zeros_like(cos)
    c = jnp.concatenate([cos, cos, z, z], axis=1)
    s1 = jnp.concatenate([z, sin, z, z], axis=1)
    s2 = jnp.concatenate([-sin, z, z, z], axis=1)
    return c, s1, s2


def _mla_mask_table():
    t = ATT_T
    r = np.arange(t)[:, None] // CHUNK
    c = np.arange(t)[None, :] // CHUNK
    return jnp.asarray(np.where(c <= r, 0.0, NEG).astype(np.float32))


def _layer(x, g_attn, w_in, g_cq, g_ckv, w_uq, w_ukv, g_mla_q, g_mla_k, g_dsa_q, g_dsa_k,
           rel_bias, w_o, g_ffn, w_pq, sub_keys, u_tab, v_tab):
    b, s, d = x.shape
    m = b * s
    bf = jnp.bfloat16
    f32 = jnp.float32
    assert s % ATT_T == 0 and m % 512 == 0

    o = np.cumsum([0, Q_LORA, KV_LORA, MLA_ROPE, 1024, 1024, 1024, 1024, IDX_HD, IDX_HEADS])
    zc = lambda n: jnp.zeros((d, n), bf)
    w_in_b = w_in.astype(bf)
    w_in_p = jnp.concatenate([
        w_in_b[:, o[3]:o[7]], w_in_b[:, o[0]:o[2]],
        w_in_b[:, o[2]:o[3]], zc(LANES - MLA_ROPE),
        w_in_b[:, o[7]:o[8]], zc(LANES - IDX_HD),
        w_in_b[:, o[8]:o[9]], zc(LANES - IDX_HEADS), zc(IN_PAD - COL_WI - LANES)], axis=1)
    wuq_p = jnp.pad(w_uq.reshape(Q_LORA, MLA_HEADS, MLA_QK),
                    ((0, 0), (0, 0), (0, MLA_QK_PAD - MLA_QK))).reshape(Q_LORA, -1).astype(bf)
    wukv_p = w_ukv.reshape(KV_LORA, MLA_HEADS, 2, MLA_NOPE).transpose(0, 2, 1, 3).reshape(KV_LORA, -1).astype(bf)
    gq_pad = jnp.pad(g_mla_q, (0, MLA_QK_PAD - MLA_QK))[None]
    gk_pad = jnp.pad(g_mla_k, (0, MLA_QK_PAD - MLA_QK))[None]
    rope_c, rope_s1, rope_s2 = _rope_tables(s)

    proj = _inproj(x.reshape(m, d), g_attn[None], w_in_p)
    proj3 = proj.reshape(b, s, IN_PAD)

    q_a, k_a, v_a = _mla_prep(proj3, g_cq[None], g_ckv[None], wuq_p, wukv_p, gq_pad, gk_pad,
                              rope_c, rope_s1, rope_s2)
    y_a = _mla_attn(q_a, k_a, v_a, _mla_mask_table())

    q_d, k_d, v_d, q_i, k_i2 = _dsa_prep(proj3, g_dsa_q[None], g_dsa_k[None])
    n_sel = min(DSA_TOPK, s // 4)
    sel = _dsa_index(q_i, proj3, k_i2, n_sel, COL_WI // LANES)
    y_b = _dsa_attn(q_d, k_d, v_d, sel, _bias_tiles(rel_bias))

    na = MLA_HEADS * MLA_V
    x1, h2t = _outproj(x.reshape(m, d), y_a.reshape(m, na), y_b.reshape(m, -1),
                      w_o[:na].astype(bf), w_o[na:].astype(bf), g_ffn[None])

    sk = sub_keys.reshape(PEER_HEADS * 2, PEER_NKEYS, -1).astype(bf)
    bs, betas, a_s = _peer_scores(h2t, w_pq.T.astype(bf), sk)
    x2 = _peer_main(h2t, u_tab.astype(bf), v_tab.T.astype(bf), bs, betas, a_s, x1)
    return x2.reshape(b, s, d)


def kernel(x, g_attn, w_in, g_cq, g_ckv, w_uq, w_ukv, g_mla_q, g_mla_k, g_dsa_q, g_dsa_k,
           rel_bias, w_o, g_ffn, w_pq, sub_keys, u_tab, v_tab):
    for l in range(w_in.shape[0]):
        x = _layer(x, g_attn[l], w_in[l], g_cq[l], g_ckv[l], w_uq[l], w_ukv[l], g_mla_q[l], g_mla_k[l],
                   g_dsa_q[l], g_dsa_k[l], rel_bias, w_o[l], g_ffn[l], w_pq[l], sub_keys[l],
                   u_tab[l], v_tab[l])
    return x
```

```python
import functools
import math

import numpy as np
import jax
import jax.numpy as jnp
from jax import lax
from jax.experimental import pallas as pl
from jax.experimental.pallas import tpu as pltpu

EPS = 1e-6
CHUNK = 64

MLA_HEADS = 8
MLA_NOPE = 128
MLA_ROPE = 64
MLA_V = 128
MLA_QK = MLA_NOPE + MLA_ROPE
MLA_QK_PAD = 256
Q_LORA = 512
KV_LORA = 256
ROPE_THETA = 10000.0

DSA_HEADS = 8
DSA_HD = 128
IDX_HEADS = 16
IDX_HD = 64
DSA_TOPK = 256

REL_BUCKETS = 32
REL_MAX_DIST = 128

PEER_HEADS = 8
PEER_NKEYS = 128
PEER_TOPK = 16

LANES = 128
NEG = -1e30
LOG2E = math.log2(math.e)
VMEM_LIMIT = 56 * 1024 * 1024

COL_Q, COL_K, COL_V, COL_QI = 0, 1024, 2048, 3072
COL_CQ, COL_CKV, COL_KR, COL_KI, COL_WI = 4096, 4608, 4864, 4992, 5120
IN_PAD = 5376

ATT_T = 256

_NT = (((1,), (1,)), ((), ()))


def _cparams(sem):
    return pltpu.CompilerParams(dimension_semantics=sem, vmem_limit_bytes=VMEM_LIMIT)


def _rep(x, n):
    return x if n == 1 else jnp.concatenate([x] * n, axis=1)


def _inproj_kernel(x_ref, g_ref, w_ref, o_ref, h_scr):
    @pl.when(pl.program_id(1) == 0)
    def _():
        x = x_ref[...]
        ms = jnp.mean(x * x, axis=-1, keepdims=True)
        h_scr[...] = (x * lax.rsqrt(ms + EPS) * g_ref[...]).astype(jnp.bfloat16)

    o_ref[...] = jnp.dot(h_scr[...], w_ref[...], preferred_element_type=jnp.float32)


def _inproj(x2d, g, w_p, tm=1024, tn=768):
    m, d = x2d.shape
    tm = min(tm, m)
    n = w_p.shape[1]
    return pl.pallas_call(
        _inproj_kernel,
        grid=(m // tm, n // tn),
        in_specs=[pl.BlockSpec((tm, d), lambda i, j: (i, 0)),
                  pl.BlockSpec((1, d), lambda i, j: (0, 0)),
                  pl.BlockSpec((d, tn), lambda i, j: (0, j))],
        out_specs=pl.BlockSpec((tm, tn), lambda i, j: (i, j)),
        out_shape=jax.ShapeDtypeStruct((m, n), jnp.float32),
        scratch_shapes=[pltpu.VMEM((tm, d), jnp.bfloat16)],
        compiler_params=_cparams(("parallel", "arbitrary")),
        name="inproj",
    )(x2d, g, w_p)


def _rope128(xr, c, s1, s2):
    return xr * c + pltpu.roll(xr, 96, 1) * s2 + pltpu.roll(xr, 32, 1) * s1


def _mla_prep_kernel(cq_ref, ckv_ref, kr_ref, gcq_ref, gckv_ref, wuq_ref, wukv_ref, gq_ref, gk_ref,
                     c_ref, s1_ref, s2_ref, q_ref, k_ref, v_ref):
    cq = cq_ref[0]
    cqn = (cq * lax.rsqrt(jnp.mean(cq * cq, axis=-1, keepdims=True) + EPS) * gcq_ref[...])
    q_all = jnp.dot(cqn.astype(jnp.bfloat16), wuq_ref[...], preferred_element_type=jnp.float32)
    ckv = ckv_ref[0]
    ckvn = (ckv * lax.rsqrt(jnp.mean(ckv * ckv, axis=-1, keepdims=True) + EPS) * gckv_ref[...])
    kv = jnp.dot(ckvn.astype(jnp.bfloat16), wukv_ref[...], preferred_element_type=jnp.float32)
    kr = kr_ref[0]
    kr_ss = jnp.sum(kr * kr, axis=-1, keepdims=True)
    c, s1, s2 = c_ref[...], s1_ref[...], s2_ref[...]
    gq, gk = gq_ref[...], gk_ref[...]
    scale = MLA_QK ** -0.5 * LOG2E
    for h in range(MLA_HEADS):
        qh = q_all[:, h * MLA_QK_PAD:(h + 1) * MLA_QK_PAD]
        rq = lax.rsqrt(jnp.sum(qh * qh, axis=-1, keepdims=True) * (1.0 / MLA_QK) + EPS)
        qn = qh * rq * gq
        q_ref[0, h, :, :MLA_NOPE] = (qn[:, :MLA_NOPE] * scale).astype(jnp.bfloat16)
        q_ref[0, h, :, MLA_NOPE:] = (_rope128(qn[:, MLA_NOPE:], c, s1, s2) * scale).astype(jnp.bfloat16)
        kn = kv[:, h * MLA_NOPE:(h + 1) * MLA_NOPE]
        rk = lax.rsqrt((jnp.sum(kn * kn, axis=-1, keepdims=True) + kr_ss) * (1.0 / MLA_QK) + EPS)
        k_ref[0, h, :, :MLA_NOPE] = (kn * rk * gk[:, :MLA_NOPE]).astype(jnp.bfloat16)
        k_ref[0, h, :, MLA_NOPE:] = _rope128(kr * rk * gk[:, MLA_NOPE:], c, s1, s2).astype(jnp.bfloat16)
        v_ref[0, h] = kv[:, MLA_HEADS * MLA_NOPE + h * MLA_V:MLA_HEADS * MLA_NOPE + (h + 1) * MLA_V
                         ].astype(jnp.bfloat16)


def _mla_prep(proj3, g_cq, g_ckv, wuq_p, wukv_p, gq_pad, gk_pad, rope_c, rope_s1, rope_s2, tm=256):
    b, s, _ = proj3.shape
    const = lambda shape: pl.BlockSpec(shape, lambda bi, i: (0,) * len(shape))
    return pl.pallas_call(
        _mla_prep_kernel,
        grid=(b, s // tm),
        in_specs=[pl.BlockSpec((1, tm, Q_LORA), lambda bi, i: (bi, i, COL_CQ // Q_LORA)),
                  pl.BlockSpec((1, tm, KV_LORA), lambda bi, i: (bi, i, COL_CKV // KV_LORA)),
                  pl.BlockSpec((1, tm, LANES), lambda bi, i: (bi, i, COL_KR // LANES)),
                  const((1, Q_LORA)), const((1, KV_LORA)),
                  const(wuq_p.shape), const(wukv_p.shape),
                  const((1, MLA_QK_PAD)), const((1, MLA_QK_PAD)),
                  pl.BlockSpec((tm, LANES), lambda bi, i: (i, 0)),
                  pl.BlockSpec((tm, LANES), lambda bi, i: (i, 0)),
                  pl.BlockSpec((tm, LANES), lambda bi, i: (i, 0))],
        out_specs=[pl.BlockSpec((1, MLA_HEADS, tm, MLA_QK_PAD), lambda bi, i: (bi, 0, i, 0)),
                   pl.BlockSpec((1, MLA_HEADS, tm, MLA_QK_PAD), lambda bi, i: (bi, 0, i, 0)),
                   pl.BlockSpec((1, MLA_HEADS, tm, MLA_V), lambda bi, i: (bi, 0, i, 0))],
        out_shape=[jax.ShapeDtypeStruct((b, MLA_HEADS, s, MLA_QK_PAD), jnp.bfloat16),
                   jax.ShapeDtypeStruct((b, MLA_HEADS, s, MLA_QK_PAD), jnp.bfloat16),
                   jax.ShapeDtypeStruct((b, MLA_HEADS, s, MLA_V), jnp.bfloat16)],
        compiler_params=_cparams(("parallel", "parallel")),
        name="mla_prep",
    )(proj3, proj3, proj3, g_cq, g_ckv, wuq_p, wukv_p, gq_pad, gk_pad, rope_c, rope_s1, rope_s2)


def _dsa_prep_kernel(q_ref, k_ref, v_ref, qi_ref, ki_ref, gq_ref, gk_ref,
                     qo_ref, ko_ref, vo_ref, qio_ref, kio_ref):
    gq, gk = gq_ref[...], gk_ref[...]
    scale = DSA_HD ** -0.5 * LOG2E
    for h in range(DSA_HEADS):
        sl = slice(h * DSA_HD, (h + 1) * DSA_HD)
        qh = q_ref[0, :, sl]
        qo_ref[0, h] = (qh * lax.rsqrt(jnp.mean(qh * qh, axis=-1, keepdims=True) + EPS) * gq * scale
                        ).astype(jnp.bfloat16)
        kh = k_ref[0, :, sl]
        ko_ref[0, h] = (kh * lax.rsqrt(jnp.mean(kh * kh, axis=-1, keepdims=True) + EPS) * gk
                        ).astype(jnp.bfloat16)
        vo_ref[0, h] = v_ref[0, :, sl].astype(jnp.bfloat16)
    qio_ref[0] = (qi_ref[0] * (IDX_HD ** -0.5)).astype(jnp.bfloat16)
    ki = ki_ref[0]
    kio_ref[0, 0] = ki.astype(jnp.bfloat16)
    kio_ref[0, 1] = pltpu.roll(ki, IDX_HD, 1).astype(jnp.bfloat16)


def _dsa_prep(proj3, gq, gk, tm=256):
    b, s, _ = proj3.shape
    w = DSA_HEADS * DSA_HD
    hd_spec = pl.BlockSpec((1, DSA_HEADS, tm, DSA_HD), lambda bi, i: (bi, 0, i, 0))
    hd_shape = jax.ShapeDtypeStruct((b, DSA_HEADS, s, DSA_HD), jnp.bfloat16)
    return pl.pallas_call(
        _dsa_prep_kernel,
        grid=(b, s // tm),
        in_specs=[pl.BlockSpec((1, tm, w), lambda bi, i: (bi, i, COL_Q // w)),
                  pl.BlockSpec((1, tm, w), lambda bi, i: (bi, i, COL_K // w)),
                  pl.BlockSpec((1, tm, w), lambda bi, i: (bi, i, COL_V // w)),
                  pl.BlockSpec((1, tm, w), lambda bi, i: (bi, i, COL_QI // w)),
                  pl.BlockSpec((1, tm, LANES), lambda bi, i: (bi, i, COL_KI // LANES)),
                  pl.BlockSpec((1, DSA_HD), lambda bi, i: (0, 0)),
                  pl.BlockSpec((1, DSA_HD), lambda bi, i: (0, 0))],
        out_specs=[hd_spec, hd_spec, hd_spec,
                   pl.BlockSpec((1, tm, w), lambda bi, i: (bi, i, 0)),
                   pl.BlockSpec((1, 2, tm, LANES), lambda bi, i: (bi, 0, i, 0))],
        out_shape=[hd_shape, hd_shape, hd_shape,
                   jax.ShapeDtypeStruct((b, s, w), jnp.bfloat16),
                   jax.ShapeDtypeStruct((b, 2, s, LANES), jnp.bfloat16)],
        compiler_params=_cparams(("parallel", "parallel")),
        name="dsa_prep",
    )(proj3, proj3, proj3, proj3, proj3, gq, gk)


def _lane_fold(x, op):
    out = x[:, :LANES]
    for c in range(1, x.shape[1] // LANES):
        out = op(out, x[:, c * LANES:(c + 1) * LANES])
    return out


ATT_SLOTS = 8


def _attn_scratch(nsub, n_tiles):
    t = ATT_T
    return ([pltpu.VMEM((nsub, t, LANES), jnp.float32)] * 3
            + [pltpu.VMEM((ATT_SLOTS, t, t), jnp.float32), pltpu.VMEM((ATT_SLOTS, t, t), jnp.bfloat16),
               pltpu.VMEM((nsub * n_tiles, t, t), jnp.float32)])


def _attn_sweep(q_ref, k_ref, v_ref, o_ref, m_scr, l_scr, acc_scr, s_scr, p_scr, all_scr, add_fn):
    qi = pl.program_id(2)
    t = ATT_T
    nsub = q_ref.shape[2] // t
    nrep = t // LANES
    nslot = s_scr.shape[0]
    first = nsub * qi

    def qk(slot, unit):
        r, ki, _ = unit
        off = pl.multiple_of(ki * t, t)
        s_scr[slot] = lax.dot_general(q_ref[0, 0, r * t:(r + 1) * t, :], k_ref[0, 0, pl.ds(off, t), :],
                                      _NT, preferred_element_type=jnp.float32)

    n_tiles = all_scr.shape[0] // nsub

    def fold_max(slot, unit):
        r, ki, _ = unit
        a = add_fn(*unit)
        s = s_scr[slot] if a is None else s_scr[slot] + a
        all_scr[r * n_tiles + ki] = s
        m_scr[r] = jnp.maximum(m_scr[r], _lane_fold(s, jnp.maximum))

    def probs(slot, unit):
        r, ki, _ = unit
        p = jnp.exp2(all_scr[r * n_tiles + ki] - _rep(m_scr[r], nrep))
        l_scr[r] += _lane_fold(p, jnp.add)
        p_scr[slot] = p.astype(jnp.bfloat16)

    def pv(slot, unit):
        r, ki, _ = unit
        off = pl.multiple_of(ki * t, t)
        acc_scr[r] += jnp.dot(p_scr[slot], v_ref[0, 0, pl.ds(off, t), :], preferred_element_type=jnp.float32)

    def run(units, stages):
        n = len(units)
        for step in range(n + len(stages) - 1):
            for k, stage in enumerate(stages):
                i = step - k
                if 0 <= i < n:
                    stage(i % nslot, units[i])

    def sweep(stages):
        def body(kp, carry):
            run([(r, 2 * kp + kk, None) for kk in range(2) for r in range(nsub)], stages)
            return carry

        if nsub % 2 == 0:
            lax.fori_loop(0, first // 2, body, 0)
        else:
            lax.fori_loop(0, first, lambda ki, c: (run([(0, ki, None)], stages), c)[1], 0)
        run([(r, first + j, j - r) for j in range(nsub) for r in range(j, nsub)], stages)

    m_scr[...] = jnp.full(m_scr.shape, NEG, jnp.float32)
    sweep((qk, fold_max))
    for r in range(nsub):
        m_scr[r] = jnp.broadcast_to(jnp.max(m_scr[r], axis=1, keepdims=True), (t, LANES))
    l_scr[...] = jnp.zeros(l_scr.shape, jnp.float32)
    acc_scr[...] = jnp.zeros(acc_scr.shape, jnp.float32)
    sweep((probs, pv))
    for r in range(nsub):
        l = jnp.sum(l_scr[r], axis=1, keepdims=True)
        o_ref[0, r * t:(r + 1) * t, :] = (acc_scr[r] / l).astype(o_ref.dtype)


def _attn_nsub(n_tiles):
    return 4 if n_tiles % 4 == 0 else (2 if n_tiles % 2 == 0 else 1)


def _mla_attn_kernel(q_ref, k_ref, v_ref, mask_ref, o_ref, *scr):
    def add_fn(r, ki, rel):
        return mask_ref[...] if rel == 0 else None

    _attn_sweep(q_ref, k_ref, v_ref, o_ref, *scr, add_fn)


def _mla_attn(q, k, v, mask_tbl):
    b, h, s, dq = q.shape
    t = ATT_T
    nsub = _attn_nsub(s // t)
    tq = nsub * t
    return pl.pallas_call(
        _mla_attn_kernel,
        grid=(b, h, s // tq),
        in_specs=[pl.BlockSpec((1, 1, tq, dq), lambda bi, hi, i: (bi, hi, i, 0)),
                  pl.BlockSpec((1, 1, s, dq), lambda bi, hi, i: (bi, hi, 0, 0)),
                  pl.BlockSpec((1, 1, s, MLA_V), lambda bi, hi, i: (bi, hi, 0, 0)),
                  pl.BlockSpec((t, t), lambda bi, hi, i: (0, 0))],
        out_specs=pl.BlockSpec((1, tq, MLA_V), lambda bi, hi, i: (bi, i, hi)),
        out_shape=jax.ShapeDtypeStruct((b, s, h * MLA_V), jnp.bfloat16),
        scratch_shapes=_attn_scratch(nsub, s // t),
        compiler_params=_cparams(("parallel", "parallel", "arbitrary")),
        name="mla_attn",
    )(q, k, v, mask_tbl)


def _dsa_attn_kernel(q_ref, k_ref, v_ref, sel_ref, bias_ref, o_ref, *scr):
    nsub = q_ref.shape[2] // ATT_T
    first = nsub * pl.program_id(2)

    def add_fn(r, ki, rel):
        back = jnp.minimum(first + r - ki, 2) if rel is None else min(-rel, 2)
        return bias_ref[0, back] + sel_ref[0, r, ki].astype(jnp.float32)

    _attn_sweep(q_ref, k_ref, v_ref, o_ref, *scr, add_fn)


def _dsa_attn(q, k, v, sel, bias_tbl):
    b, h, s, d = q.shape
    t = ATT_T
    n = s // t
    nsub = _attn_nsub(n)
    tq = nsub * t
    return pl.pallas_call(
        _dsa_attn_kernel,
        grid=(b, h, s // tq),
        in_specs=[pl.BlockSpec((1, 1, tq, d), lambda bi, hi, i: (bi, hi, i, 0)),
                  pl.BlockSpec((1, 1, s, d), lambda bi, hi, i: (bi, hi, 0, 0)),
                  pl.BlockSpec((1, 1, s, d), lambda bi, hi, i: (bi, hi, 0, 0)),
                  pl.BlockSpec((1, nsub, n, t, t), lambda bi, hi, i: (bi, i, 0, 0, 0)),
                  pl.BlockSpec((1, 3, t, t), lambda bi, hi, i: (hi, 0, 0, 0))],
        out_specs=pl.BlockSpec((1, tq, d), lambda bi, hi, i: (bi, i, hi)),
        out_shape=jax.ShapeDtypeStruct((b, s, h * d), jnp.bfloat16),
        scratch_shapes=_attn_scratch(nsub, s // t),
        compiler_params=_cparams(("parallel", "parallel", "arbitrary")),
        name="dsa_attn",
    )(q, k, v, sel, bias_tbl)


def _bias_kernel(rb_ref, o_ref):
    h = pl.program_id(0)
    t = o_ref.shape[2]
    row = lax.broadcasted_iota(jnp.int32, (t, t), 0)
    col = lax.broadcasted_iota(jnp.int32, (t, t), 1)
    nb = REL_BUCKETS // 2
    max_exact = nb // 2
    for r in range(2):
        rel = col - row - r * t
        n = jnp.abs(rel)
        n2 = n * n
        large = jnp.full((t, t), max_exact, jnp.int32)
        for j in range(1, nb - max_exact):
            large = large + (n2 >= (max_exact * max_exact) * (2 ** j)).astype(jnp.int32)
        bucket = jnp.where(rel > 0, nb, 0) + jnp.where(n < max_exact, n, large)
        val = jnp.zeros((t, t), jnp.float32)
        for bkt in range(REL_BUCKETS):
            val = jnp.where(bucket == bkt, rb_ref[bkt, h], val)
        o_ref[0, r] = val * LOG2E
    o_ref[0, 2] = jnp.full((t, t), rb_ref[nb - 1, h] * LOG2E, jnp.float32)


def _bias_tiles(rel_bias):
    t = ATT_T
    assert t >= REL_MAX_DIST
    return pl.pallas_call(
        _bias_kernel,
        grid=(DSA_HEADS,),
        in_specs=[pl.BlockSpec(memory_space=pltpu.SMEM)],
        out_specs=pl.BlockSpec((1, 3, t, t), lambda h: (h, 0, 0, 0)),
        out_shape=jax.ShapeDtypeStruct((DSA_HEADS, 3, t, t), jnp.float32),
        compiler_params=_cparams(("arbitrary",)),
        name="t5_bias",
    )(rel_bias)


_F32_INF_BITS = 0x7F800000
_INT_MIN = np.int32(-2 ** 31)


def _index_kernel(n_sel, q_ref, w_ref, k_ref, o_ref, score_scr, wt_scr):
    qi = pl.program_id(1)
    t = q_ref.shape[1]
    n_tiles = score_scr.shape[0]
    n_adm = qi + 1
    wt_scr[...] = w_ref[0].T
    q_chunk = (qi * t + lax.broadcasted_iota(jnp.int32, (1, t), 1)) // CHUNK

    def score_body(c, carry):
        off = pl.multiple_of(c * t, t)
        acc = jnp.zeros((t, t), jnp.float32)
        for hp in range(IDX_HEADS // 2):
            qpair = q_ref[0, :, hp * LANES:(hp + 1) * LANES]
            for par in range(2):
                h = 2 * hp + par
                kk = k_ref[0, par, pl.ds(off, t), :]
                d = lax.dot_general(kk, qpair, _NT, preferred_element_type=jnp.float32)
                acc = acc + jnp.maximum(d, 0.0) * wt_scr[h:h + 1, :]
        k_chunk = (c * t + lax.broadcasted_iota(jnp.int32, (t, 1), 0)) // CHUNK
        score_scr[c] = jnp.where(k_chunk <= q_chunk, acc, -jnp.inf)
        return carry

    lax.fori_loop(0, n_adm, score_body, 0)

    def count_ge(cand):
        def cnt_body(c, cnt):
            ge = jnp.where(score_scr[c] >= cand, 1.0, 0.0)
            return cnt + jnp.sum(ge.reshape(4, t // 32, 8, t), axis=1)

        cnt = lax.fori_loop(0, n_adm, cnt_body, jnp.zeros((4, 8, t), jnp.float32))
        return jnp.sum(jnp.sum(cnt, axis=0), axis=0, keepdims=True)

    kf = float(n_sel)
    pos = count_ge(jnp.zeros((1, t), jnp.float32)) >= kf

    def thr_of(u):
        neg_bits = jnp.bitwise_or(_F32_INF_BITS - u, _INT_MIN)
        return pltpu.bitcast(jnp.where(pos, u, neg_bits), jnp.float32)

    def bit_body(it, u):
        cand = u + jnp.left_shift(jnp.int32(1), 30 - it)
        ok = jnp.logical_and(cand <= _F32_INF_BITS, count_ge(thr_of(cand)) >= kf)
        return jnp.where(ok, cand, u)

    thr = thr_of(lax.fori_loop(0, 31, bit_body, jnp.zeros((1, t), jnp.int32)))

    def out_body(c, carry):
        score = score_scr[c]
        sel = jnp.logical_and(score >= thr, score > -jnp.inf)
        o_ref[0, 0, c] = jnp.where(sel, 0.0, NEG).T.astype(o_ref.dtype)
        return carry

    lax.fori_loop(0, n_adm, out_body, 0)

    def fill_body(c, carry):
        o_ref[0, 0, c] = jnp.full((t, t), NEG, o_ref.dtype)
        return carry

    lax.fori_loop(n_adm, n_tiles, fill_body, 0)


def _dsa_index(q_idx, w3, k_idx2, n_sel, w_col_block):
    b, s, _ = q_idx.shape
    t = ATT_T
    n = s // t
    return pl.pallas_call(
        functools.partial(_index_kernel, n_sel),
        grid=(b, n),
        in_specs=[pl.BlockSpec((1, t, IDX_HEADS * IDX_HD), lambda bi, i: (bi, i, 0)),
                  pl.BlockSpec((1, t, LANES), lambda bi, i: (bi, i, w_col_block)),
                  pl.BlockSpec((1, 2, s, LANES), lambda bi, i: (bi, 0, 0, 0))],
        out_specs=pl.BlockSpec((1, 1, n, t, t), lambda bi, i: (bi, i, 0, 0, 0)),
        out_shape=jax.ShapeDtypeStruct((b, n, n, t, t), jnp.bfloat16),
        scratch_shapes=[pltpu.VMEM((n, t, t), jnp.float32),
                        pltpu.VMEM((LANES, t), jnp.float32)],
        compiler_params=_cparams(("parallel", "parallel")),
        name="dsa_index",
    )(q_idx, w3, k_idx2)


def _outproj_kernel(x_ref, ya_ref, yb_ref, wa_ref, wb_ref, g_ref, x1_ref, h2t_ref):
    y = (jnp.dot(ya_ref[...], wa_ref[...], preferred_element_type=jnp.float32)
         + jnp.dot(yb_ref[...], wb_ref[...], preferred_element_type=jnp.float32))
    x1 = x_ref[...] + y
    x1_ref[...] = x1
    ms = jnp.mean(x1 * x1, axis=-1, keepdims=True)
    h2t_ref[...] = (x1 * lax.rsqrt(ms + EPS) * g_ref[...]).T.astype(jnp.bfloat16)


def _outproj(x2d, ya, yb, wo_a, wo_b, g_ffn, tm=256):
    m, d = x2d.shape
    ka, kb = ya.shape[1], yb.shape[1]
    return pl.pallas_call(
        _outproj_kernel,
        grid=(m // tm,),
        in_specs=[pl.BlockSpec((tm, d), lambda i: (i, 0)),
                  pl.BlockSpec((tm, ka), lambda i: (i, 0)),
                  pl.BlockSpec((tm, kb), lambda i: (i, 0)),
                  pl.BlockSpec((ka, d), lambda i: (0, 0)),
                  pl.BlockSpec((kb, d), lambda i: (0, 0)),
                  pl.BlockSpec((1, d), lambda i: (0, 0))],
        out_specs=[pl.BlockSpec((tm, d), lambda i: (i, 0)),
                   pl.BlockSpec((d, tm), lambda i: (0, i))],
        out_shape=[jax.ShapeDtypeStruct((m, d), jnp.float32),
                   jax.ShapeDtypeStruct((d, m), jnp.bfloat16)],
        compiler_params=_cparams(("parallel",)),
        name="outproj",
    )(x2d, ya, yb, wo_a, wo_b, g_ffn)


def _sort_network(n):
    def merge(lo, hi, r):
        step = r * 2
        if step < hi - lo:
            yield from merge(lo, hi, step)
            yield from merge(lo + r, hi, step)
            yield from [(i, i + r) for i in range(lo + r, hi - r, step)]
        else:
            yield (lo, lo + r)

    def sort(lo, hi):
        if hi - lo >= 1:
            mid = lo + (hi - lo) // 2
            yield from sort(lo, mid)
            yield from sort(mid + 1, hi)
            yield from merge(lo, hi, 1)

    return list(sort(0, n - 1))


SUBLANES = 8


def _top_values(x, k):
    ngroups = x.shape[0] // SUBLANES
    rows = [x[SUBLANES * i:SUBLANES * (i + 1)] for i in range(ngroups)]
    for i, j in _sort_network(ngroups):
        rows[i], rows[j] = jnp.maximum(rows[i], rows[j]), jnp.minimum(rows[i], rows[j])
    out = []
    for r in range(k):
        m = jnp.max(rows[0], axis=0, keepdims=True)
        out.append(m)
        live = min(ngroups, k - 1 - r)
        pop = rows[0] == m
        for lvl in range(live):
            below = rows[lvl + 1] if lvl + 1 < ngroups else jnp.full_like(rows[lvl], -jnp.inf)
            rows[lvl] = jnp.where(pop, below, rows[lvl])
    return out


_N_TOP = PEER_TOPK + 1
_CAND_PAIRS = [(a, b) for a in range(_N_TOP) for b in range(_N_TOP) if (a + 1) * (b + 1) <= _N_TOP]


def _peer_score_kernel(h_ref, wq_ref, sk_ref, b_ref, beta_ref, a_ref):
    tm = h_ref.shape[1]
    qr_t = jnp.dot(wq_ref[...], h_ref[...], preferred_element_type=jnp.float32)
    for h in range(PEER_HEADS):
        q0 = qr_t[(2 * h) * PEER_NKEYS:(2 * h + 1) * PEER_NKEYS].astype(jnp.bfloat16)
        q1 = qr_t[(2 * h + 1) * PEER_NKEYS:(2 * h + 2) * PEER_NKEYS].astype(jnp.bfloat16)
        s0 = jnp.dot(sk_ref[2 * h], q0, preferred_element_type=jnp.float32)
        s1 = jnp.dot(sk_ref[2 * h + 1], q1, preferred_element_type=jnp.float32)
        top0 = _top_values(s0, _N_TOP)
        top1 = _top_values(s1, _N_TOP)
        cands = [top0[a] + top1[b] for (a, b) in _CAND_PAIRS]
        pad = (-len(cands)) % (8 * SUBLANES)
        cands += [jnp.full((1, tm), -jnp.inf, jnp.float32)] * pad
        vals = _top_values(jnp.concatenate(cands, axis=0), _N_TOP)
        z = jnp.zeros((1, tm), jnp.float32)
        for v in vals[:PEER_TOPK]:
            z = z + jnp.exp(v - vals[0])
        cut = 0.5 * (vals[PEER_TOPK - 1] + vals[PEER_TOPK])
        b_ref[h] = jnp.exp(s1 - top1[0]).astype(b_ref.dtype)
        beta_ref[h] = jnp.exp((cut - top1[0]) - s0)
        a_ref[h] = jnp.exp(s0 - top0[0]) / z


def _peer_scores(h2t, wpq_t, sk, tm=256):
    d, m = h2t.shape
    nq = wpq_t.shape[0]
    out_spec = pl.BlockSpec((PEER_HEADS, PEER_NKEYS, tm), lambda i: (0, 0, i))
    out_shape = jax.ShapeDtypeStruct((PEER_HEADS, PEER_NKEYS, m), jnp.float32)
    return pl.pallas_call(
        _peer_score_kernel,
        grid=(m // tm,),
        in_specs=[pl.BlockSpec((d, tm), lambda i: (0, i)),
                  pl.BlockSpec((nq, d), lambda i: (0, 0)),
                  pl.BlockSpec(sk.shape, lambda i: (0, 0, 0))],
        out_specs=[out_spec, out_spec, out_spec],
        out_shape=[jax.ShapeDtypeStruct(out_shape.shape, jnp.bfloat16), out_shape, out_shape],
        compiler_params=_cparams(("parallel",)),
        name="peer_scores",
    )(h2t, wpq_t, sk)


def _gelu_exact(x):
    return 0.5 * x * (1.0 + lax.erf(x * (1.0 / math.sqrt(2.0))))


PEER_KC = 256


def _peer_main_step(h_ref, u_ref, vt_ref, b_ref, beta_ref, a_ref, acc_scr, w_scr, act_next, act_cur):
    tm = h_ref.shape[1]
    te = u_ref.shape[0]
    if act_next is not None:
        act_next[...] = jnp.dot(u_ref[...], h_ref[...], preferred_element_type=jnp.float32)
    if act_cur is None:
        return
    parts = []
    for c in range(te // PEER_KC):
        for il in range(c * PEER_KC // PEER_NKEYS, (c + 1) * PEER_KC // PEER_NKEYS):
            rows = slice(il * PEER_NKEYS, (il + 1) * PEER_NKEYS)
            for lt in range(tm // LANES):
                lanes = slice(lt * LANES, (lt + 1) * LANES)
                g = None
                for h in range(PEER_HEADS):
                    b = b_ref[h, :, lanes]
                    beta = beta_ref[h, il:il + 1, lanes].astype(jnp.bfloat16)
                    a = a_ref[h, il:il + 1, lanes].astype(jnp.bfloat16)
                    term = a * jnp.where(b >= beta, b, jnp.zeros_like(b))
                    g = term if g is None else g + term
                w_scr[rows, lanes] = g * _gelu_exact(act_cur[rows, lanes]).astype(jnp.bfloat16)
        ck = slice(c * PEER_KC, (c + 1) * PEER_KC)
        parts.append(jnp.dot(vt_ref[:, ck], w_scr[ck, :], preferred_element_type=jnp.float32))
    acc_scr[...] += functools.reduce(lambda p, q: p + q, parts)


def _peer_main_kernel(last, h_ref, u_ref, vt_ref, b_ref, beta_ref, a_ref, x1_ref, o_ref,
                      acc_scr, w_scr, act_a, act_b):
    e = pl.program_id(1)
    args = (h_ref, u_ref, vt_ref, b_ref, beta_ref, a_ref, acc_scr, w_scr)
    slots = (act_a, act_b)

    @pl.when(e == 0)
    def _():
        acc_scr[...] = jnp.zeros(acc_scr.shape, jnp.float32)
        _peer_main_step(*args, slots[0], None)

    for parity in range(2):
        @pl.when(jnp.logical_and(e % 2 == parity, jnp.logical_and(e > 0, e < last)))
        def _():
            _peer_main_step(*args, slots[parity], slots[1 - parity])

    @pl.when(e == last)
    def _():
        _peer_main_step(*args, None, slots[(last - 1) % 2])
        o_ref[...] = x1_ref[...] + acc_scr[...].T


def _peer_main(h2t, u_bf, vt_bf, bs, betas, a_s, x1, tm=512, te=1024):
    d, m = h2t.shape
    ne = u_bf.shape[0] // te
    ni = te // PEER_NKEYS
    back = lambda e, n: jnp.clip(e - n, 0, ne - 1)
    once = pl.Buffered(2)
    return pl.pallas_call(
        functools.partial(_peer_main_kernel, ne),
        grid=(m // tm, ne + 1),
        in_specs=[pl.BlockSpec((d, tm), lambda i, e: (0, i), pipeline_mode=once),
                  pl.BlockSpec((te, d), lambda i, e: (back(e, 0), 0)),
                  pl.BlockSpec((d, te), lambda i, e: (0, back(e, 1))),
                  pl.BlockSpec((PEER_HEADS, PEER_NKEYS, tm), lambda i, e: (0, 0, i), pipeline_mode=once),
                  pl.BlockSpec((PEER_HEADS, ni, tm), lambda i, e: (0, back(e, 1), i)),
                  pl.BlockSpec((PEER_HEADS, ni, tm), lambda i, e: (0, back(e, 1), i)),
                  pl.BlockSpec((tm, d), lambda i, e: (i, 0), pipeline_mode=once)],
        out_specs=pl.BlockSpec((tm, d), lambda i, e: (i, 0)),
        out_shape=jax.ShapeDtypeStruct((m, d), jnp.float32),
        scratch_shapes=[pltpu.VMEM((d, tm), jnp.float32),
                        pltpu.VMEM((te, tm), jnp.bfloat16),
                        pltpu.VMEM((te, tm), jnp.float32),
                        pltpu.VMEM((te, tm), jnp.float32)],
        compiler_params=_cparams(("parallel", "arbitrary")),
        name="peer_main",
    )(h2t, u_bf, vt_bf, bs, betas, a_s, x1)


def _rope_tables(s):
    half = MLA_ROPE // 2
    freqs = ROPE_THETA ** (-2.0 * jnp.arange(half, dtype=jnp.float32) / MLA_ROPE)
    ang = jnp.arange(s, dtype=jnp.float32)[:, None] * freqs[None, :]
    cos, sin = jnp.cos(ang), jnp.sin(ang)
    z = jnp.zeros_like(cos)
    c = jnp.concatenate([cos, cos, z, z], axis=1)
    s1 = jnp.concatenate([z, sin, z, z], axis=1)
    s2 = jnp.concatenate([-sin, z, z, z], axis=1)
    return c, s1, s2


def _mla_mask_table():
    t = ATT_T
    r = np.arange(t)[:, None] // CHUNK
    c = np.arange(t)[None, :] // CHUNK
    return jnp.asarray(np.where(c <= r, 0.0, NEG).astype(np.float32))


def _layer(x, g_attn, w_in, g_cq, g_ckv, w_uq, w_ukv, g_mla_q, g_mla_k, g_dsa_q, g_dsa_k,
           rel_bias, w_o, g_ffn, w_pq, sub_keys, u_tab, v_tab):
    b, s, d = x.shape
    m = b * s
    bf = jnp.bfloat16
    f32 = jnp.float32
    assert s % ATT_T == 0 and m % 512 == 0

    o = np.cumsum([0, Q_LORA, KV_LORA, MLA_ROPE, 1024, 1024, 1024, 1024, IDX_HD, IDX_HEADS])
    zc = lambda n: jnp.zeros((d, n), bf)
    w_in_b = w_in.astype(bf)
    w_in_p = jnp.concatenate([
        w_in_b[:, o[3]:o[7]], w_in_b[:, o[0]:o[2]],
        w_in_b[:, o[2]:o[3]], zc(LANES - MLA_ROPE),
        w_in_b[:, o[7]:o[8]], zc(LANES - IDX_HD),
        w_in_b[:, o[8]:o[9]], zc(LANES - IDX_HEADS), zc(IN_PAD - COL_WI - LANES)], axis=1)
    wuq_p = jnp.pad(w_uq.reshape(Q_LORA, MLA_HEADS, MLA_QK),
                    ((0, 0), (0, 0), (0, MLA_QK_PAD - MLA_QK))).reshape(Q_LORA, -1).astype(bf)
    wukv_p = w_ukv.reshape(KV_LORA, MLA_HEADS, 2, MLA_NOPE).transpose(0, 2, 1, 3).reshape(KV_LORA, -1).astype(bf)
    gq_pad = jnp.pad(g_mla_q, (0, MLA_QK_PAD - MLA_QK))[None]
    gk_pad = jnp.pad(g_mla_k, (0, MLA_QK_PAD - MLA_QK))[None]
    rope_c, rope_s1, rope_s2 = _rope_tables(s)

    proj = _inproj(x.reshape(m, d), g_attn[None], w_in_p)
    proj3 = proj.reshape(b, s, IN_PAD)

    q_a, k_a, v_a = _mla_prep(proj3, g_cq[None], g_ckv[None], wuq_p, wukv_p, gq_pad, gk_pad,
                              rope_c, rope_s1, rope_s2)
    y_a = _mla_attn(q_a, k_a, v_a, _mla_mask_table())

    q_d, k_d, v_d, q_i, k_i2 = _dsa_prep(proj3, g_dsa_q[None], g_dsa_k[None])
    n_sel = min(DSA_TOPK, s // 4)
    sel = _dsa_index(q_i, proj3, k_i2, n_sel, COL_WI // LANES)
    y_b = _dsa_attn(q_d, k_d, v_d, sel, _bias_tiles(rel_bias))

    na = MLA_HEADS * MLA_V
    x1, h2t = _outproj(x.reshape(m, d), y_a.reshape(m, na), y_b.reshape(m, -1),
                      w_o[:na].astype(bf), w_o[na:].astype(bf), g_ffn[None])

    sk = sub_keys.reshape(PEER_HEADS * 2, PEER_NKEYS, -1).astype(bf)
    bs, betas, a_s = _peer_scores(h2t, w_pq.T.astype(bf), sk)
    x2 = _peer_main(h2t, u_tab.astype(bf), v_tab.T.astype(bf), bs, betas, a_s, x1)
    return x2.reshape(b, s, d)


def kernel(x, g_attn, w_in, g_cq, g_ckv, w_uq, w_ukv, g_mla_q, g_mla_k, g_dsa_q, g_dsa_k,
           rel_bias, w_o, g_ffn, w_pq, sub_keys, u_tab, v_tab):
    for l in range(w_in.shape[0]):
        x = _layer(x, g_attn[l], w_in[l], g_cq[l], g_ckv[l], w_uq[l], w_ukv[l], g_mla_q[l], g_mla_k[l],
                   g_dsa_q[l], g_dsa_k[l], rel_bias, w_o[l], g_ffn[l], w_pq[l], sub_keys[l],
                   u_tab[l], v_tab[l])
    return x
```
